```python
import math
import jax, jax.numpy as jnp
from jax import lax
import numpy as np

D_MODEL = 1024
BATCH = 8
SEQ = 4096
DEPTH = 2

HEAD_DIM = 64
ROPE_THETA = 500000.0
ROPE_FRAC = 4
QBLK = 128
EPS = 1e-6

DIFF_HEADS = 4
DIFF_DC = HEAD_DIM // 2
DIFF_DV = HEAD_DIM
MLSTM_HEADS = 4
MLSTM_DQK = HEAD_DIM
MLSTM_DV = HEAD_DIM
MLSTM_CHUNK = 64
CONV_W = 4
SB_HEADS = 4
SB_D = HEAD_DIM
DSA_HEADS = 4
DSA_D = HEAD_DIM
IDX_HEADS = 8
IDX_D = 32
DSA_TOPK_MAX = 256
N_BRANCH = 4
BRANCH_W = 4 * HEAD_DIM
D_FF = 2816

IN_COLS = (
    ("diff_q", DIFF_HEADS * 2 * DIFF_DC),
    ("diff_k", DIFF_HEADS * 2 * DIFF_DC),
    ("diff_v", DIFF_HEADS * DIFF_DV),
    ("ml_qk", 2 * MLSTM_HEADS * MLSTM_DQK),
    ("ml_v", MLSTM_HEADS * MLSTM_DV),
    ("ml_i", MLSTM_HEADS),
    ("ml_f", MLSTM_HEADS),
    ("ml_o", MLSTM_HEADS * MLSTM_DV),
    ("sb_q", SB_HEADS * SB_D),
    ("sb_k", SB_HEADS * SB_D),
    ("sb_v", SB_HEADS * SB_D),
    ("dsa_q", DSA_HEADS * DSA_D),
    ("dsa_k", DSA_D),
    ("dsa_v", DSA_D),
    ("idx_q", IDX_HEADS * IDX_D),
    ("idx_k", IDX_D),
    ("idx_w", IDX_HEADS),
    ("gates", N_BRANCH * D_MODEL),
)
N_IN = sum(w for _, w in IN_COLS)

kernel_name = "hybrid_gated_diff_mlstm_stickbreak_dsa"


def rmsnorm(x, g):
    xf = x.astype(jnp.float32)
    y = xf * lax.rsqrt(jnp.mean(xf * xf, axis=-1, keepdims=True) + EPS)
    return (y * g.astype(jnp.float32)).astype(x.dtype)


def rope(x, pos, rot_dim):
    half = rot_dim // 2
    inv = ROPE_THETA ** (-jnp.arange(half, dtype=jnp.float32) / half)
    ang = pos.astype(jnp.float32)[..., None] * inv
    ang = ang.reshape(ang.shape[:2] + (1,) * (x.ndim - 3) + (half,))
    cos, sin = jnp.cos(ang).astype(x.dtype), jnp.sin(ang).astype(x.dtype)
    x1, x2, rest = x[..., :half], x[..., half:rot_dim], x[..., rot_dim:]
    return jnp.concatenate([x1 * cos - x2 * sin, x1 * sin + x2 * cos, rest], axis=-1)


def split_cols(z):
    out, off = {}, 0
    for name, w in IN_COLS:
        out[name] = z[..., off:off + w]
        off += w
    return out


def causal_mask(q0, q1, strict):
    qpos = jnp.arange(q0, q1)[:, None]
    kpos = jnp.arange(q1)[None, :]
    return (kpos < qpos) if strict else (kpos <= qpos)


def sweep(block_fn, seq):
    return jnp.concatenate([block_fn(q0, q0 + QBLK) for q0 in range(0, seq, QBLK)], axis=1)


def swiglu(h, w_gu, w_down):
    g, u = jnp.split(h @ w_gu, 2, axis=-1)
    return (jax.nn.silu(g) * u) @ w_down


def causal_conv(x, w, b):
    y = lax.conv_general_dilated(x, w[:, None, :].astype(x.dtype), window_strides=(1,),
                                 padding=[(CONV_W - 1, 0)],
                                 dimension_numbers=("NWC", "WIO", "NWC"),
                                 feature_group_count=x.shape[-1])
    return y + b


def diff_attention(c, pos, qk_g, lam_p, head_g, layer_idx):
    B, S, _ = c["diff_q"].shape
    q = c["diff_q"].reshape(B, S, DIFF_HEADS, 2, DIFF_DC)
    k = c["diff_k"].reshape(B, S, DIFF_HEADS, 2, DIFF_DC)
    v = c["diff_v"].reshape(B, S, DIFF_HEADS, DIFF_DV)
    q = rope(rmsnorm(q, qk_g[0]), pos, DIFF_DC // ROPE_FRAC)
    k = rope(rmsnorm(k, qk_g[1]), pos, DIFF_DC // ROPE_FRAC)
    lam_init = 0.8 - 0.6 * math.exp(-0.3 * layer_idx)
    lp = lam_p.astype(jnp.float32)
    lam = jnp.exp(jnp.sum(lp[0] * lp[1])) - jnp.exp(jnp.sum(lp[2] * lp[3])) + lam_init
    scale = DIFF_DC ** -0.5

    def block(q0, q1):
        s = jnp.einsum("bqhcd,bkhcd->bhcqk", q[:, q0:q1], k[:, :q1]).astype(jnp.float32) * scale
        p = jax.nn.softmax(jnp.where(causal_mask(q0, q1, False), s, -jnp.inf), axis=-1)
        a = p[:, :, 0] - lam * p[:, :, 1]
        return jnp.einsum("bhqk,bkhd->bqhd", a.astype(v.dtype), v[:, :q1])

    o = rmsnorm(sweep(block, S), head_g) * (1.0 - lam_init)
    return o.reshape(B, S, DIFF_HEADS * DIFF_DV)


def mlstm_chunkwise(q, k, v, ig, lf):
    B, S, H, DK = q.shape
    DV = v.shape[-1]
    L = MLSTM_CHUNK
    NC = S // L

    def chunks(a):
        a = a.reshape((B, NC, L, H) + a.shape[3:])
        return jnp.moveaxis(a, (1, 3), (0, 2))

    tri = jnp.tril(jnp.ones((L, L), dtype=bool))

    def step(carry, inp):
        C, n, m = carry
        qb, kb, vb, ib, fb = inp
        b = jnp.cumsum(fb, axis=-1)
        dl = jnp.where(tri, b[..., :, None] - b[..., None, :] + ib[..., None, :], -jnp.inf)
        inter = b + m[..., None]
        mt = jnp.maximum(inter, jnp.max(dl, axis=-1))
        dw = jnp.exp(dl - mt[..., None])
        iw = jnp.exp(inter - mt)
        s = jnp.einsum("bhtd,bhsd->bhts", qb, kb) * dw
        num = iw[..., None] * jnp.einsum("bhtd,bhde->bhte", qb, C) + jnp.einsum("bhts,bhse->bhte", s, vb)
        den = iw * jnp.einsum("bhtd,bhd->bht", qb, n) + jnp.sum(s, axis=-1)
        h = num / jnp.maximum(jnp.abs(den), jnp.exp(-mt))[..., None]
        bl = b[..., -1]
        g = bl[..., None] - b + ib
        m_new = jnp.maximum(bl + m, jnp.max(g, axis=-1))
        decay = jnp.exp(bl + m - m_new)
        wk = jnp.exp(g - m_new[..., None])
        C = decay[..., None, None] * C + jnp.einsum("bhs,bhsd,bhse->bhde", wk, kb, vb)
        n = decay[..., None] * n + jnp.einsum("bhs,bhsd->bhd", wk, kb)
        return (C, n, m_new), h

    f32 = jnp.float32
    init = (jnp.zeros((B, H, DK, DV), f32), jnp.zeros((B, H, DK), f32), jnp.zeros((B, H), f32))
    _, hs = lax.scan(step, init, (chunks(q.astype(f32)), chunks(k.astype(f32)), chunks(v.astype(f32)),
                                  chunks(ig), chunks(lf)))
    return jnp.moveaxis(hs, (0, 2), (1, 3)).reshape(B, S, H, DV)


def mlstm(c, conv_w, conv_b, gate_b, head_g):
    B, S, _ = c["ml_v"].shape
    qk = jax.nn.silu(causal_conv(c["ml_qk"], conv_w, conv_b))
    q, k = jnp.split(qk, 2, axis=-1)
    q = q.reshape(B, S, MLSTM_HEADS, MLSTM_DQK) * (MLSTM_DQK ** -0.5)
    k = k.reshape(B, S, MLSTM_HEADS, MLSTM_DQK)
    v = c["ml_v"].reshape(B, S, MLSTM_HEADS, MLSTM_DV)
    gb = gate_b.astype(jnp.float32)
    ig = c["ml_i"].astype(jnp.float32) + gb[0]
    lf = jax.nn.log_sigmoid(c["ml_f"].astype(jnp.float32) + gb[1])
    h = mlstm_chunkwise(q, k, v, ig, lf).astype(c["ml_v"].dtype)
    o = jax.nn.sigmoid(c["ml_o"]).reshape(B, S, MLSTM_HEADS, MLSTM_DV)
    return (o * rmsnorm(h, head_g)).reshape(B, S, MLSTM_HEADS * MLSTM_DV)


def stick_breaking(c):
    B, S, _ = c["sb_q"].shape
    q = c["sb_q"].reshape(B, S, SB_HEADS, SB_D)
    k = c["sb_k"].reshape(B, S, SB_HEADS, SB_D)
    v = c["sb_v"].reshape(B, S, SB_HEADS, SB_D)
    scale = SB_D ** -0.5

    def block(q0, q1):
        z = jnp.einsum("bqhd,bkhd->bhqk", q[:, q0:q1], k[:, :q1]).astype(jnp.float32) * scale
        mask = causal_mask(q0, q1, True)
        log_1m = jnp.where(mask, jax.nn.log_sigmoid(-z), 0.0)
        after = lax.cumsum(log_1m, axis=3, reverse=True) - log_1m
        a = jnp.where(mask, jnp.exp(jax.nn.log_sigmoid(z) + after), 0.0)
        return jnp.einsum("bhqk,bkhd->bqhd", a.astype(v.dtype), v[:, :q1])

    return sweep(block, S).reshape(B, S, SB_HEADS * SB_D)


def dsa(c, pos, qk_g):
    B, S, _ = c["dsa_q"].shape
    q = rope(rmsnorm(c["dsa_q"].reshape(B, S, DSA_HEADS, DSA_D), qk_g[0]), pos, DSA_D // ROPE_FRAC)
    k = rope(rmsnorm(c["dsa_k"], qk_g[1]), pos, DSA_D // ROPE_FRAC)
    v = c["dsa_v"]
    qi = rope(c["idx_q"].reshape(B, S, IDX_HEADS, IDX_D), pos, IDX_D // ROPE_FRAC)
    ki = rope(c["idx_k"], pos, IDX_D // ROPE_FRAC)
    wi = c["idx_w"]
    topk = min(DSA_TOPK_MAX, S // 4)
    scale = DSA_D ** -0.5
    gather = jax.vmap(lambda a, i: a[i])

    def block(q0, q1):
        qpos = jnp.arange(q0, q1)
        r = jax.nn.relu(jnp.einsum("bqhd,bkd->bqhk", qi[:, q0:q1], ki[:, :q1]).astype(jnp.float32))
        score = jnp.einsum("bqh,bqhk->bqk", wi[:, q0:q1].astype(jnp.float32), r)
        score = jnp.where(jnp.arange(q1)[None, None, :] <= qpos[None, :, None], score, -jnp.inf)
        _, sel = lax.top_k(score, min(topk, q1))
        valid = sel <= qpos[None, :, None]
        ks, vs = gather(k, sel), gather(v, sel)
        s = jnp.einsum("bqhd,bqnd->bhqn", q[:, q0:q1], ks).astype(jnp.float32) * scale
        p = jax.nn.softmax(jnp.where(valid[:, None], s, -jnp.inf), axis=-1)
        return jnp.einsum("bhqn,bqnd->bqhd", p.astype(v.dtype), vs)

    return sweep(block, S).reshape(B, S, DSA_HEADS * DSA_D)


def setup_inputs(seed: int = 0) -> dict:
    key = jax.random.key(seed)
    ks = jax.random.split(key, 24)
    nrm = lambda k, shape, s: jax.random.normal(k, shape, jnp.float32) * s
    gain = lambda k, shape: 1.0 + 0.01 * jax.random.normal(k, shape, jnp.float32)
    ml_w = 2 * MLSTM_HEADS * MLSTM_DQK
    i_bias = nrm(ks[10], (DEPTH, MLSTM_HEADS), 0.1)
    f_bias = jnp.linspace(3.0, 6.0, MLSTM_HEADS)[None, :] + nrm(ks[11], (DEPTH, MLSTM_HEADS), 0.1)
    positions = (jnp.arange(SEQ, dtype=jnp.int32)[None, :]
                 + jax.random.randint(ks[1], (BATCH, 1), 0, 1024, dtype=jnp.int32))
    return {
        "x": nrm(ks[0], (BATCH, SEQ, D_MODEL), 1.0),
        "positions": positions,
        "ffn1_norm": gain(ks[2], (DEPTH, D_MODEL)),
        "ffn1_w_gu": nrm(ks[3], (DEPTH, D_MODEL, 2 * D_FF), D_MODEL ** -0.5),
        "ffn1_w_down": nrm(ks[4], (DEPTH, D_FF, D_MODEL), D_FF ** -0.5),
        "mix_norm": gain(ks[5], (DEPTH, D_MODEL)),
        "w_in": nrm(ks[6], (DEPTH, D_MODEL, N_IN), D_MODEL ** -0.5),
        "diff_qk_norm": gain(ks[7], (DEPTH, 2, DIFF_DC)),
        "diff_lambda": nrm(ks[8], (DEPTH, 4, DIFF_DC), 0.1),
        "diff_head_norm": gain(ks[9], (DEPTH, DIFF_DV)),
        "ml_conv_w": nrm(ks[12], (DEPTH, CONV_W, ml_w), CONV_W ** -0.5),
        "ml_conv_b": nrm(ks[13], (DEPTH, ml_w), 0.01),
        "ml_gate_bias": jnp.stack([i_bias, f_bias], axis=1),
        "ml_head_norm": gain(ks[14], (DEPTH, MLSTM_DV)),
        "dsa_qk_norm": gain(ks[15], (DEPTH, 2, DSA_D)),
        "w_branch": nrm(ks[16], (DEPTH, N_BRANCH, BRANCH_W, D_MODEL), BRANCH_W ** -0.5),
        "w_out": nrm(ks[17], (DEPTH, D_MODEL, D_MODEL), D_MODEL ** -0.5),
        "ffn2_norm": gain(ks[18], (DEPTH, D_MODEL)),
        "ffn2_w_gu": nrm(ks[19], (DEPTH, D_MODEL, 2 * D_FF), D_MODEL ** -0.5),
        "ffn2_w_down": nrm(ks[20], (DEPTH, D_FF, D_MODEL), D_FF ** -0.5),
    }


def reference(x, positions, ffn1_norm, ffn1_w_gu, ffn1_w_down, mix_norm, w_in, diff_qk_norm,
              diff_lambda, diff_head_norm, ml_conv_w, ml_conv_b, ml_gate_bias, ml_head_norm,
              dsa_qk_norm, w_branch, w_out, ffn2_norm, ffn2_w_gu, ffn2_w_down):
    B, S, _ = x.shape
    for l in range(DEPTH):
        x = x + 0.5 * swiglu(rmsnorm(x, ffn1_norm[l]), ffn1_w_gu[l], ffn1_w_down[l])
        h = rmsnorm(x, mix_norm[l])
        c = split_cols(h @ w_in[l])
        outs = (
            diff_attention(c, positions, diff_qk_norm[l], diff_lambda[l], diff_head_norm[l], l),
            mlstm(c, ml_conv_w[l], ml_conv_b[l], ml_gate_bias[l], ml_head_norm[l]),
            stick_breaking(c),
            dsa(c, positions, dsa_qk_norm[l]),
        )
        gates = jax.nn.sigmoid(c["gates"]).reshape(B, S, N_BRANCH, D_MODEL)
        y = gates[:, :, 0] * (outs[0] @ w_branch[l, 0])
        for bi in range(1, N_BRANCH):
            y = y + gates[:, :, bi] * (outs[bi] @ w_branch[l, bi])
        x = x + y @ w_out[l]
        x = x + 0.5 * swiglu(rmsnorm(x, ffn2_norm[l]), ffn2_w_gu[l], ffn2_w_down[l])
    return x
```

```python
import functools
import math

import numpy as np
import jax
import jax.numpy as jnp
from jax import lax
from jax.experimental import pallas as pl
from jax.experimental.pallas import tpu as pltpu

F32 = jnp.float32
BF16 = jnp.bfloat16
I32 = jnp.int32

EPS = 1e-6
ROPE_THETA = 500000.0
ROPE_FRAC = 4
HEAD_DIM = 64
N_HEADS = 4
BRANCH_W = N_HEADS * HEAD_DIM
DIFF_DC = HEAD_DIM // 2
IDX_HEADS = 8
IDX_D = 32
DSA_TOPK_MAX = 256
CONV_W = 4
N_BRANCH = 4
NEG_BIG = -1e30
INT_MIN = -(2 ** 31)
LANE = 128
VMEM_LIMIT = 56 * 1024 * 1024

Z_DIFF_Q, Z_DIFF_K, Z_DIFF_V = 0, 256, 512
Z_ML_V, Z_ML_QK, Z_ML_O = 768, 1024, 1536
Z_SB_Q, Z_SB_K, Z_SB_V = 1792, 2048, 2304
Z_DSA_Q, Z_IDX_Q = 2560, 2816
Z_SM_A, Z_SM_B = 3072, 3200
Z_COLS = 3328
SM_GATE = 0
SM_IDXW = 4
SM_IDXK = 32
SM_KV = 64


def _in_col_permutation(d_model):
    src, off = {}, 0
    for name, w in (("diff_q", 256), ("diff_k", 256), ("diff_v", 256), ("ml_qk", 512), ("ml_v", 256),
                    ("ml_i", 4), ("ml_f", 4), ("ml_o", 256), ("sb_q", 256), ("sb_k", 256), ("sb_v", 256),
                    ("dsa_q", 256), ("dsa_k", 64), ("dsa_v", 64), ("idx_q", 256), ("idx_k", 32),
                    ("idx_w", 8), ("gates", N_BRANCH * d_model)):
        src[name] = (off, w)
        off += w
    perm = -np.ones((Z_COLS,), np.int64)

    def put(dst, name):
        o, w = src[name]
        perm[dst:dst + w] = np.arange(o, o + w)

    put(Z_DIFF_Q, "diff_q"); put(Z_DIFF_K, "diff_k"); put(Z_DIFF_V, "diff_v")
    put(Z_ML_QK, "ml_qk"); put(Z_ML_V, "ml_v"); put(Z_ML_O, "ml_o")
    put(Z_SB_Q, "sb_q"); put(Z_SB_K, "sb_k"); put(Z_SB_V, "sb_v")
    put(Z_DSA_Q, "dsa_q"); put(Z_IDX_Q, "idx_q")
    put(Z_SM_A + SM_GATE, "ml_i"); put(Z_SM_A + SM_IDXK, "idx_k"); put(Z_SM_A + SM_KV, "dsa_k")
    put(Z_SM_B + SM_GATE, "ml_f"); put(Z_SM_B + SM_IDXW, "idx_w"); put(Z_SM_B + SM_KV, "dsa_v")
    return perm, src["gates"][0]


def _rope_rows(width, group, rot_dim):
    half = rot_dim // 2
    inv = ROPE_THETA ** (-np.arange(half, dtype=np.float32) / half)
    r = np.arange(width) % group
    invf = np.where(r < rot_dim, inv[r % half], 0.0).astype(np.float32)
    m_lo = np.where(r < half, -1.0, 0.0).astype(np.float32)
    m_hi = np.where((r >= half) & (r < rot_dim), 1.0, 0.0).astype(np.float32)
    return jnp.asarray(np.stack([invf, m_lo, m_hi]))


def _group_mean_matrix(width, group):
    g = np.arange(width) // group
    return jnp.asarray((g[:, None] == g[None, :]).astype(np.float32) / group, dtype=BF16)


def _dot(a, b):
    return jnp.dot(a, b, preferred_element_type=F32)


def _dot_nt(a, b):
    return lax.dot_general(a, b, (((1,), (1,)), ((), ())), preferred_element_type=F32)


def _dot_split(a, b):
    hi = a.astype(BF16)
    lo = (a - hi.astype(F32)).astype(BF16)
    return _dot(hi, b) + _dot(lo, b)


def _rms(x, g):
    return x * lax.rsqrt(jnp.mean(x * x, axis=-1, keepdims=True) + EPS) * g


def _softplus(z):
    return jnp.maximum(z, 0.0) + jnp.log(1.0 + jnp.exp(-jnp.abs(z)))


def _group_norm(x, gmat, gain):
    ms = _dot_split(x * x, gmat)
    return x * lax.rsqrt(ms + EPS) * gain


def _rope(x, pos, rope_rows, half):
    w = x.shape[-1]
    ang = pos * rope_rows[0:1, :]
    cos, sin = jnp.cos(ang), jnp.sin(ang)
    partner = (pltpu.roll(x, w - half, 1) * rope_rows[1:2, :] + pltpu.roll(x, half, 1) * rope_rows[2:3, :])
    return x * cos + partner * sin


def _cparams(sem):
    return pltpu.CompilerParams(dimension_semantics=sem, vmem_limit_bytes=VMEM_LIMIT)


def _full(shape):
    n = len(shape)
    return pl.BlockSpec(shape, lambda *_: (0,) * n)


def _ffn_kernel(x_ref, g_ref, wg_ref, wu_ref, wd_ref, o_ref, h_ref, acc_ref, *, n_chunks):
    h_ref[...] = _rms(x_ref[...], g_ref[...]).astype(BF16)
    acc_ref[...] = jnp.zeros_like(acc_ref)

    def body(c, carry):
        h = h_ref[...]
        g = _dot(h, wg_ref[c])
        u = _dot(h, wu_ref[c])
        a = (g * jax.nn.sigmoid(g)) * u
        acc_ref[...] += _dot(a.astype(BF16), wd_ref[c])
        return carry

    lax.fori_loop(0, n_chunks, body, 0)
    o_ref[...] = x_ref[...] + 0.5 * acc_ref[...]


def _ffn(x2, g, wg3, wu3, wd3, tm):
    t, d = x2.shape
    n_chunks, _, tf = wg3.shape
    return pl.pallas_call(
        functools.partial(_ffn_kernel, n_chunks=n_chunks),
        grid=(t // tm,),
        in_specs=[pl.BlockSpec((tm, d), lambda i: (i, 0)), _full((1, d)),
                  _full(wg3.shape), _full(wu3.shape), _full(wd3.shape)],
        out_specs=pl.BlockSpec((tm, d), lambda i: (i, 0)),
        out_shape=jax.ShapeDtypeStruct((t, d), F32),
        scratch_shapes=[pltpu.VMEM((tm, d), BF16), pltpu.VMEM((tm, d), F32)],
        compiler_params=_cparams(("parallel",)),
        name="ffn",
    )(x2, g, wg3, wu3, wd3)


def _mixin_kernel(x_ref, g_ref, w_ref, z_ref, *, col_chunk):
    h = _rms(x_ref[...], g_ref[...]).astype(BF16)
    for c in range(0, z_ref.shape[-1], col_chunk):
        z_ref[:, c:c + col_chunk] = _dot(h, w_ref[:, c:c + col_chunk])


def _mixin(x2, g, w_perm, tm):
    t, d = x2.shape
    n = w_perm.shape[1]
    return pl.pallas_call(
        functools.partial(_mixin_kernel, col_chunk=256),
        grid=(t // tm,),
        in_specs=[pl.BlockSpec((tm, d), lambda i: (i, 0)), _full((1, d)), _full((d, n))],
        out_specs=pl.BlockSpec((tm, n), lambda i: (i, 0)),
        out_shape=jax.ShapeDtypeStruct((t, n), F32),
        compiler_params=_cparams(("parallel",)),
        name="mix_in",
    )(x2, g, w_perm)


def _merge_kernel(x_ref, g_ref, o0_ref, o1_ref, o2_ref, o3_ref, wgate_ref, wbr_ref, wout_ref, out_ref):
    x = x_ref[...]
    h = _rms(x, g_ref[...]).astype(BF16)
    y = None
    for b, o_ref in enumerate((o0_ref, o1_ref, o2_ref, o3_ref)):
        gate = jax.nn.sigmoid(_dot(h, wgate_ref[b]))
        yb = gate * _dot(o_ref[...], wbr_ref[b])
        y = yb if y is None else y + yb
    out_ref[...] = x + _dot(y.astype(BF16), wout_ref[...])


def _merge(x2, g, outs, wgate, wbr, wout, tm):
    t, d = x2.shape
    bw = wbr.shape[1]
    o_spec = pl.BlockSpec((tm, bw), lambda i: (i, 0))
    return pl.pallas_call(
        _merge_kernel,
        grid=(t // tm,),
        in_specs=[pl.BlockSpec((tm, d), lambda i: (i, 0)), _full((1, d)), o_spec, o_spec, o_spec, o_spec,
                  _full(wgate.shape), _full(wbr.shape), _full(wout.shape)],
        out_specs=pl.BlockSpec((tm, d), lambda i: (i, 0)),
        out_shape=jax.ShapeDtypeStruct((t, d), F32),
        compiler_params=_cparams(("parallel",)),
        name="merge",
    )(x2, g, *outs, wgate, wbr, wout)


def _diff_kernel(pos_ref, zq_ref, zk_ref, zv_ref, gq_ref, gk_ref, lp_ref, hg_ref, rope_ref, gmat_ref,
                 o_ref, k_scr, v_scr, *, tq, seq, prep_rows, lam_init):
    qi = pl.program_id(1)
    half = DIFF_DC // ROPE_FRAC // 2
    rope_rows = rope_ref[...]
    gmat = gmat_ref[...]

    @pl.when(qi == 0)
    def _prep():
        def body(c, carry):
            r0 = pl.multiple_of(c * prep_rows, prep_rows)
            rows = pl.ds(r0, prep_rows)
            k = _group_norm(zk_ref[0, rows, :], gmat, gk_ref[...])
            k_scr[rows, :] = _rope(k, pos_ref[0, rows, :], rope_rows, half).astype(BF16)
            v_scr[rows, :] = zv_ref[0, rows, :].astype(BF16)
            return carry
        lax.fori_loop(0, seq // prep_rows, body, 0)

    q0 = pl.multiple_of(qi * tq, tq)
    q = _group_norm(zq_ref[0], gmat, gq_ref[...])
    q = _rope(q, pos_ref[0, pl.ds(q0, tq), :], rope_rows, half) * (DIFF_DC ** -0.5)
    q = q.astype(BF16)

    lp = lp_ref[...]
    lam = (jnp.exp(jnp.sum(lp[0:1] * lp[1:2], axis=-1, keepdims=True))
           - jnp.exp(jnp.sum(lp[2:3] * lp[3:4], axis=-1, keepdims=True)) + lam_init)

    row = lax.broadcasted_iota(I32, (tq, tq), 0)
    col = lax.broadcasted_iota(I32, (tq, tq), 1)

    for h in range(N_HEADS):
        lo = h * HEAD_DIM
        qa = q[:, lo:lo + DIFF_DC]
        qb = q[:, lo + DIFF_DC:lo + HEAD_DIM]

        def body(j, carry, lo=lo, qa=qa, qb=qb):
            k0 = pl.multiple_of(j * tq, tq)
            kblk = k_scr[pl.ds(k0, tq), lo:lo + HEAD_DIM]
            vblk = v_scr[pl.ds(k0, tq), lo:lo + HEAD_DIM]
            mask = (col + (j - qi) * tq) <= row
            new = []
            for (m, l, acc), qc, kc in ((carry[0], qa, kblk[:, :DIFF_DC]), (carry[1], qb, kblk[:, DIFF_DC:])):
                s = jnp.where(mask, _dot_nt(qc, kc), NEG_BIG)
                m_new = jnp.maximum(m, jnp.max(s, axis=-1, keepdims=True))
                alpha = jnp.exp(m - m_new)
                p = jnp.exp(s - m_new)
                l = alpha * l + jnp.sum(p, axis=-1, keepdims=True)
                acc = alpha * acc + _dot(p.astype(BF16), vblk)
                new.append((m_new, l, acc))
            return tuple(new)

        init = (jnp.full((tq, 1), NEG_BIG, F32), jnp.zeros((tq, 1), F32), jnp.zeros((tq, HEAD_DIM), F32))
        (_, l0, a0), (_, l1, a1) = lax.fori_loop(0, qi + 1, body, (init, init))
        o = a0 / l0 - lam * (a1 / l1)
        o = _rms(o, hg_ref[...]) * (1.0 - lam_init)
        o_ref[0, :, lo:lo + HEAD_DIM] = o.astype(o_ref.dtype)


def _diff_attention(z, pos, qk_g, lam_p, head_g, layer_idx, tq):
    b, s, _ = z.shape
    lam_init = 0.8 - 0.6 * math.exp(-0.3 * layer_idx)
    reps = BRANCH_W // DIFF_DC
    gq = jnp.tile(qk_g[0], reps)[None, :]
    gk = jnp.tile(qk_g[1], reps)[None, :]
    rope_rows = _rope_rows(BRANCH_W, DIFF_DC, DIFF_DC // ROPE_FRAC)
    gmat = _group_mean_matrix(BRANCH_W, DIFF_DC)
    blk = BRANCH_W
    return pl.pallas_call(
        functools.partial(_diff_kernel, tq=tq, seq=s, prep_rows=min(512, s), lam_init=lam_init),
        grid=(b, s // tq),
        in_specs=[pl.BlockSpec((1, s, 1), lambda bi, qi: (bi, 0, 0)),
                  pl.BlockSpec((1, tq, blk), lambda bi, qi: (bi, qi, Z_DIFF_Q // blk)),
                  pl.BlockSpec((1, s, blk), lambda bi, qi: (bi, 0, Z_DIFF_K // blk)),
                  pl.BlockSpec((1, s, blk), lambda bi, qi: (bi, 0, Z_DIFF_V // blk)),
                  _full((1, blk)), _full((1, blk)), _full(lam_p.shape), _full((1, HEAD_DIM)),
                  _full(rope_rows.shape), _full(gmat.shape)],
        out_specs=pl.BlockSpec((1, tq, blk), lambda bi, qi: (bi, qi, 0)),
        out_shape=jax.ShapeDtypeStruct((b, s, blk), BF16),
        scratch_shapes=[pltpu.VMEM((s, blk), BF16), pltpu.VMEM((s, blk), BF16)],
        compiler_params=_cparams(("parallel", "arbitrary")),
        name="diff_attn",
    )(pos, z, z, z, gq, gk, lam_p, head_g[None, :], rope_rows, gmat)


def _sb_kernel(zq_ref, zk_ref, zv_ref, o_ref, k_scr, v_scr, *, tq, seq, prep_rows):
    qi = pl.program_id(1)

    @pl.when(qi == 0)
    def _prep():
        def body(c, carry):
            rows = pl.ds(pl.multiple_of(c * prep_rows, prep_rows), prep_rows)
            k_scr[rows, :] = zk_ref[0, rows, :].astype(BF16)
            v_scr[rows, :] = zv_ref[0, rows, :].astype(BF16)
            return carry
        lax.fori_loop(0, seq // prep_rows, body, 0)

    q = (zq_ref[0] * (HEAD_DIM ** -0.5)).astype(BF16)
    row = lax.broadcasted_iota(I32, (tq, tq), 0)
    col = lax.broadcasted_iota(I32, (tq, tq), 1)
    later = (row > col).astype(BF16)

    for h in range(N_HEADS):
        lo = h * HEAD_DIM
        qh = q[:, lo:lo + HEAD_DIM]

        def body(jj, carry, lo=lo, qh=qh):
            tail, acc = carry
            j = qi - jj
            k0 = pl.multiple_of(j * tq, tq)
            kblk = k_scr[pl.ds(k0, tq), lo:lo + HEAD_DIM]
            vblk = v_scr[pl.ds(k0, tq), lo:lo + HEAD_DIM]
            z = _dot_nt(qh, kblk)
            mask = (col + (j - qi) * tq) < row
            sp = _softplus(z)
            log_1m = jnp.where(mask, -sp, 0.0)
            inside = _dot_split(log_1m, later)
            a = jnp.where(mask, jnp.exp(z - sp + inside + tail), 0.0)
            acc = acc + _dot(a.astype(BF16), vblk)
            tail = tail + inside[:, 0:1] + log_1m[:, 0:1]
            return tail, acc

        init = (jnp.zeros((tq, 1), F32), jnp.zeros((tq, HEAD_DIM), F32))
        _, acc = lax.fori_loop(0, qi + 1, body, init)
        o_ref[0, :, lo:lo + HEAD_DIM] = acc.astype(o_ref.dtype)


def _stick_breaking(z, tq):
    b, s, _ = z.shape
    blk = BRANCH_W
    return pl.pallas_call(
        functools.partial(_sb_kernel, tq=tq, seq=s, prep_rows=min(512, s)),
        grid=(b, s // tq),
        in_specs=[pl.BlockSpec((1, tq, blk), lambda bi, qi: (bi, qi, Z_SB_Q // blk)),
                  pl.BlockSpec((1, s, blk), lambda bi, qi: (bi, 0, Z_SB_K // blk)),
                  pl.BlockSpec((1, s, blk), lambda bi, qi: (bi, 0, Z_SB_V // blk))],
        out_specs=pl.BlockSpec((1, tq, blk), lambda bi, qi: (bi, qi, 0)),
        out_shape=jax.ShapeDtypeStruct((b, s, blk), BF16),
        scratch_shapes=[pltpu.VMEM((s, blk), BF16), pltpu.VMEM((s, blk), BF16)],
        compiler_params=_cparams(("parallel", "arbitrary")),
        name="stick_breaking",
    )(z, z, z)


def _mlstm_kernel(zqk_ref, zv_ref, zo_ref, za_ref, zb_ref, cw_ref, cb_ref, bi_ref, bf_ref, hg_ref,
                  o_ref, xbuf, c_scr, m_scr, *, chunk):
    ci = pl.program_id(1)
    L = chunk
    pad = 8

    @pl.when(ci == 0)
    def _init():
        xbuf[0:pad, :] = jnp.zeros((pad, xbuf.shape[1]), F32)
        c_scr[...] = jnp.zeros_like(c_scr)
        m_scr[...] = jnp.zeros_like(m_scr)

    x = zqk_ref[0]
    xbuf[pad:pad + L, :] = x
    y = x * cw_ref[CONV_W - 1:CONV_W, :] + cb_ref[...]
    for d in range(1, CONV_W):
        y = y + xbuf[pad - d:pad - d + L, :] * cw_ref[CONV_W - 1 - d:CONV_W - d, :]
    xbuf[0:pad, :] = x[L - pad:L, :]
    qk = y * jax.nn.sigmoid(y)
    q_all = (qk[:, :BRANCH_W] * (HEAD_DIM ** -0.5)).astype(BF16)
    k_all = qk[:, BRANCH_W:]

    ig = za_ref[0] + bi_ref[...]
    fpre = zb_ref[0] + bf_ref[...]
    lf = -_softplus(-fpre)
    row = lax.broadcasted_iota(I32, (L, L), 0)
    col = lax.broadcasted_iota(I32, (L, L), 1)
    tri = row >= col
    bcum = _dot_split_left(tri.astype(BF16), lf)
    cmat = ig - bcum
    cmat_t = cmat.T

    v_all = zv_ref[0]
    lane = lax.broadcasted_iota(I32, (L, LANE), 1)
    ogate = jax.nn.sigmoid(zo_ref[0])

    for h in range(N_HEADS):
        lo = h * HEAD_DIM
        b_col = bcum[:, h:h + 1]
        c_row = cmat_t[h:h + 1, :]
        c_col = cmat[:, h:h + 1]
        m_old = m_scr[h:h + 1, 0:1]
        qh = q_all[:, lo:lo + HEAD_DIM]
        kh = k_all[:, lo:lo + HEAD_DIM]
        v2 = v_all[:, (h // 2) * LANE:(h // 2 + 1) * LANE]
        if h % 2 == 1:
            v2 = pltpu.roll(v2, HEAD_DIM, 1)
        v_aug = jnp.where(lane < HEAD_DIM, v2, (lane == HEAD_DIM).astype(F32)).astype(BF16)

        dl = jnp.where(tri, b_col + c_row, NEG_BIG)
        inter = b_col + m_old
        mt = jnp.maximum(inter, jnp.max(dl, axis=-1, keepdims=True))
        dw = jnp.exp(dl - mt)
        iw = jnp.exp(inter - mt)
        s = _dot_nt(qh, kh.astype(BF16)) * dw
        state = c_scr[h]
        qc = _dot(qh, state.astype(BF16))
        sv = _dot(s.astype(BF16), v_aug)
        num = iw * qc[:, :HEAD_DIM] + sv[:, :HEAD_DIM]
        den = iw * qc[:, HEAD_DIM:HEAD_DIM + 1] + jnp.sum(s, axis=-1, keepdims=True)
        hout = num / jnp.maximum(jnp.abs(den), jnp.exp(-mt))

        bl = b_col[L - 1:L, :]
        g = bl + c_col
        m_new = jnp.maximum(bl + m_old, jnp.max(g, axis=0, keepdims=True))
        decay = jnp.exp(bl + m_old - m_new)
        wk = jnp.exp(g - m_new)
        kw_t = _transpose_cols(kh * wk)
        c_scr[h] = decay * state + _dot(kw_t.astype(BF16), v_aug)
        m_scr[h:h + 1, :] = jnp.broadcast_to(m_new, (1, LANE))

        hn = _rms(hout, hg_ref[...])
        o_ref[0, :, lo:lo + HEAD_DIM] = (ogate[:, lo:lo + HEAD_DIM] * hn).astype(o_ref.dtype)


def _dot_split_left(a, b):
    hi = b.astype(BF16)
    lo = (b - hi.astype(F32)).astype(BF16)
    return _dot(a, hi) + _dot(a, lo)


def _transpose_cols(x):
    l, w = x.shape
    xp = jnp.concatenate([x, jnp.zeros((l, LANE - w), x.dtype)], axis=1)
    return xp.T[:w, :]


def _mlstm(z, conv_w, conv_b, gate_b, head_g, chunk):
    b, s, _ = z.shape
    bi = jnp.zeros((1, LANE), F32).at[0, SM_GATE:SM_GATE + N_HEADS].set(gate_b[0])
    bf = jnp.zeros((1, LANE), F32).at[0, SM_GATE:SM_GATE + N_HEADS].set(gate_b[1])
    wqk = 2 * BRANCH_W
    return pl.pallas_call(
        functools.partial(_mlstm_kernel, chunk=chunk),
        grid=(b, s // chunk),
        in_specs=[pl.BlockSpec((1, chunk, wqk), lambda bi_, ci: (bi_, ci, Z_ML_QK // wqk)),
                  pl.BlockSpec((1, chunk, BRANCH_W), lambda bi_, ci: (bi_, ci, Z_ML_V // BRANCH_W)),
                  pl.BlockSpec((1, chunk, BRANCH_W), lambda bi_, ci: (bi_, ci, Z_ML_O // BRANCH_W)),
                  pl.BlockSpec((1, chunk, LANE), lambda bi_, ci: (bi_, ci, Z_SM_A // LANE)),
                  pl.BlockSpec((1, chunk, LANE), lambda bi_, ci: (bi_, ci, Z_SM_B // LANE)),
                  _full(conv_w.shape), _full((1, wqk)), _full((1, LANE)), _full((1, LANE)), _full((1, HEAD_DIM))],
        out_specs=pl.BlockSpec((1, chunk, BRANCH_W), lambda bi_, ci: (bi_, ci, 0)),
        out_shape=jax.ShapeDtypeStruct((b, s, BRANCH_W), BF16),
        scratch_shapes=[pltpu.VMEM((chunk + 8, wqk), F32), pltpu.VMEM((N_HEADS, HEAD_DIM, LANE), F32),
                        pltpu.VMEM((8, LANE), F32)],
        compiler_params=_cparams(("parallel", "arbitrary")),
        name="mlstm",
    )(z, z, z, z, z, conv_w, conv_b[None, :], bi, bf, head_g[None, :])


def _dsa_key_rope_rows():
    lane = np.arange(LANE)
    rows = np.zeros((5, LANE), np.float32)
    for base, rot, mrow in ((SM_IDXK, IDX_D // ROPE_FRAC, 1), (SM_KV, HEAD_DIM // ROPE_FRAC, 3)):
        half = rot // 2
        inv = ROPE_THETA ** (-np.arange(half, dtype=np.float32) / half)
        r = lane - base
        inside = (r >= 0) & (r < rot)
        rows[0] = np.where(inside, inv[np.clip(r, 0, rot - 1) % half], rows[0])
        rows[mrow] = np.where((r >= 0) & (r < half), -1.0, 0.0)
        rows[mrow + 1] = np.where((r >= half) & (r < rot), 1.0, 0.0)
    return jnp.asarray(rows)


def _sortable_key(score):
    bits = lax.bitcast_convert_type(jnp.where(score == 0.0, 0.0, score), I32)
    return bits ^ (lax.shift_right_arithmetic(bits, 31) & 0x7FFFFFFF)


def _dsa_kernel(pos_ref, zq_ref, ziq_ref, za_ref, zb_ref, gq_ref, gk_ref, ropeq_ref, ropei_ref, ropek_ref,
                gmat_ref, o_ref, ka_scr, vb_scr, keys_scr, *, tq, seq, prep_rows, topk, idx_bits):
    qi = pl.program_id(1)
    nblk = qi + 1
    half_q = HEAD_DIM // ROPE_FRAC // 2
    half_i = IDX_D // ROPE_FRAC // 2

    @pl.when(qi == 0)
    def _prep():
        lane = lax.broadcasted_iota(I32, (prep_rows, LANE), 1)
        is_k = lane >= SM_KV
        rk = ropek_ref[...]

        def body(c, carry):
            rows = pl.ds(pl.multiple_of(c * prep_rows, prep_rows), prep_rows)
            a = za_ref[0, rows, :]
            ms = jnp.sum(jnp.where(is_k, a * a, 0.0), axis=-1, keepdims=True) * (1.0 / HEAD_DIM)
            a = jnp.where(is_k, a * lax.rsqrt(ms + EPS) * gk_ref[...], a)
            ang = pos_ref[0, rows, :] * rk[0:1, :]
            partner = (pltpu.roll(a, LANE - half_i, 1) * rk[1:2, :] + pltpu.roll(a, half_i, 1) * rk[2:3, :]
                       + pltpu.roll(a, LANE - half_q, 1) * rk[3:4, :] + pltpu.roll(a, half_q, 1) * rk[4:5, :])
            ka_scr[rows, :] = (a * jnp.cos(ang) + partner * jnp.sin(ang)).astype(BF16)
            vb_scr[rows, :] = zb_ref[0, rows, :].astype(BF16)
            return carry
        lax.fori_loop(0, seq // prep_rows, body, 0)

    q0 = pl.multiple_of(qi * tq, tq)
    posq = pos_ref[0, pl.ds(q0, tq), :]
    lane = lax.broadcasted_iota(I32, (tq, LANE), 1)

    q = _group_norm(zq_ref[0], gmat_ref[...], gq_ref[...])
    q = _rope(q, posq, ropeq_ref[...], half_q) * (HEAD_DIM ** -0.5)
    parts = []
    for h in range(N_HEADS):
        part = q[:, (h // 2) * LANE:(h // 2 + 1) * LANE]
        if h % 2 == 0:
            part = pltpu.roll(part, SM_KV, 1)
        parts.append(jnp.where(lane >= SM_KV, part, 0.0).astype(BF16))
    qs = jnp.concatenate(parts, axis=0)

    qx = _rope(ziq_ref[0], posq, ropei_ref[...], half_i)
    per = LANE // IDX_D
    parts = []
    for h in range(IDX_HEADS):
        part = qx[:, (h // per) * LANE:(h // per + 1) * LANE]
        shift = (SM_IDXK - IDX_D * (h % per)) % LANE
        if shift:
            part = pltpu.roll(part, shift, 1)
        parts.append(jnp.where((lane >= SM_IDXK) & (lane < SM_IDXK + IDX_D), part, 0.0).astype(BF16))
    qis = jnp.concatenate(parts, axis=0)
    wrow = zb_ref[0, pl.ds(q0, tq), :]
    w3 = jnp.stack([wrow[:, SM_IDXW + h:SM_IDXW + h + 1] for h in range(IDX_HEADS)], axis=0)

    row = lax.broadcasted_iota(I32, (tq, tq), 0)
    col = lax.broadcasted_iota(I32, (tq, tq), 1)

    def score_body(j, carry):
        kb = ka_scr[pl.ds(pl.multiple_of(j * tq, tq), tq), :]
        r = jnp.maximum(_dot_nt(qis, kb), 0.0).reshape(IDX_HEADS, tq, tq)
        score = jnp.sum(r * w3, axis=0)
        causal = (col + (j - qi) * tq) <= row
        keys_scr[j] = jnp.where(causal, _sortable_key(score), INT_MIN)
        return carry
    lax.fori_loop(0, nblk, score_body, 0)

    def count(pred):
        def body(j, acc):
            return acc + jnp.where(pred(keys_scr[j], j), 1.0, 0.0)
        acc = lax.fori_loop(0, nblk, body, jnp.zeros((tq, tq), F32))
        return jnp.sum(acc, axis=-1, keepdims=True)

    def thr_body(it, t_u):
        cand_u = t_u | lax.shift_left(jnp.int32(1), 31 - it)
        cand = cand_u ^ INT_MIN
        return jnp.where(count(lambda kb, j: kb >= cand) >= topk, cand_u, t_u)
    thr = lax.fori_loop(0, 32, thr_body, jnp.zeros((tq, 1), I32)) ^ INT_MIN

    n_gt = count(lambda kb, j: kb > thr)
    n_ge = count(lambda kb, j: kb >= thr)
    need = topk - n_gt
    has_thr = thr > INT_MIN
    excess = has_thr & ((n_ge - n_gt) > need)

    def tie_limit():
        def body(it, x):
            cand = x | lax.shift_left(jnp.int32(1), idx_bits - 1 - it)
            below = count(lambda kb, j: (kb == thr) & ((col + j * tq) < cand))
            return jnp.where(below < need, cand, x)
        return lax.fori_loop(0, idx_bits, body, jnp.zeros((tq, 1), I32))

    any_excess = jnp.max(jnp.where(excess, 1.0, 0.0)) > 0.0
    limit = lax.cond(any_excess, tie_limit, lambda: jnp.zeros((tq, 1), I32))
    limit = jnp.where(excess, limit, jnp.where(has_thr, seq, -1))

    def att_body(j, carry):
        m, l, acc = carry
        rows = pl.ds(pl.multiple_of(j * tq, tq), tq)
        kb = keys_scr[j]
        sel = (kb > thr) | ((kb == thr) & ((col + j * tq) <= limit))
        s = _dot_nt(qs, ka_scr[rows, :]).reshape(N_HEADS, tq, tq)
        s = jnp.where(sel[None], s, NEG_BIG)
        m_new = jnp.maximum(m, jnp.max(s, axis=-1, keepdims=True))
        alpha = jnp.exp(m - m_new)
        p = jnp.where(sel[None], jnp.exp(s - m_new), 0.0)
        l = alpha * l + jnp.sum(p, axis=-1, keepdims=True)
        pv = _dot(p.reshape(N_HEADS * tq, tq).astype(BF16), vb_scr[rows, :])
        return m_new, l, alpha * acc + pv.reshape(N_HEADS, tq, LANE)

    init = (jnp.full((N_HEADS, tq, 1), NEG_BIG, F32), jnp.zeros((N_HEADS, tq, 1), F32),
            jnp.zeros((N_HEADS, tq, LANE), F32))
    _, l, acc = lax.fori_loop(0, nblk, att_body, init)
    out = acc / l
    for h in range(N_HEADS):
        o_ref[0, :, h * HEAD_DIM:(h + 1) * HEAD_DIM] = out[h][:, SM_KV:SM_KV + HEAD_DIM].astype(o_ref.dtype)


def _dsa(z, pos, qk_g, tq):
    b, s, _ = z.shape
    topk = min(DSA_TOPK_MAX, s // 4)
    gq = jnp.tile(qk_g[0], N_HEADS)[None, :]
    gk = jnp.zeros((1, LANE), F32).at[0, SM_KV:SM_KV + HEAD_DIM].set(qk_g[1])
    rope_q = _rope_rows(BRANCH_W, HEAD_DIM, HEAD_DIM // ROPE_FRAC)
    rope_i = _rope_rows(IDX_HEADS * IDX_D, IDX_D, IDX_D // ROPE_FRAC)
    rope_k = _dsa_key_rope_rows()
    gmat = _group_mean_matrix(BRANCH_W, HEAD_DIM)
    blk = BRANCH_W
    return pl.pallas_call(
        functools.partial(_dsa_kernel, tq=tq, seq=s, prep_rows=min(512, s), topk=float(topk),
                          idx_bits=max(1, (s - 1).bit_length())),
        grid=(b, s // tq),
        in_specs=[pl.BlockSpec((1, s, 1), lambda bi, qi: (bi, 0, 0)),
                  pl.BlockSpec((1, tq, blk), lambda bi, qi: (bi, qi, Z_DSA_Q // blk)),
                  pl.BlockSpec((1, tq, blk), lambda bi, qi: (bi, qi, Z_IDX_Q // blk)),
                  pl.BlockSpec((1, s, LANE), lambda bi, qi: (bi, 0, Z_SM_A // LANE)),
                  pl.BlockSpec((1, s, LANE), lambda bi, qi: (bi, 0, Z_SM_B // LANE)),
                  _full((1, blk)), _full((1, LANE)), _full(rope_q.shape), _full(rope_i.shape),
                  _full(rope_k.shape), _full(gmat.shape)],
        out_specs=pl.BlockSpec((1, tq, blk), lambda bi, qi: (bi, qi, 0)),
        out_shape=jax.ShapeDtypeStruct((b, s, blk), BF16),
        scratch_shapes=[pltpu.VMEM((s, LANE), BF16), pltpu.VMEM((s, LANE), BF16),
                        pltpu.VMEM((s // tq, tq, tq), I32)],
        compiler_params=_cparams(("parallel", "arbitrary")),
        name="dsa",
    )(pos, z, z, z, z, gq, gk, rope_q, rope_i, rope_k, gmat)


def _ffn_weights(w_gu, w_down, tf):
    d, two_f = w_gu.shape
    f = two_f // 2
    nc = f // tf
    wg3 = w_gu[:, :f].reshape(d, nc, tf).transpose(1, 0, 2).astype(BF16)
    wu3 = w_gu[:, f:].reshape(d, nc, tf).transpose(1, 0, 2).astype(BF16)
    wd3 = w_down.reshape(nc, tf, d).astype(BF16)
    return wg3, wu3, wd3


def _tiles(seq):
    return dict(tm=min(512, seq), tq_diff=min(256, seq), tq_sb=min(256, seq), tq_dsa=min(128, seq),
                chunk=min(256, seq))


def kernel(x, positions, ffn1_norm, ffn1_w_gu, ffn1_w_down, mix_norm, w_in, diff_qk_norm, diff_lambda, diff_head_norm, ml_conv_w, ml_conv_b, ml_gate_bias, ml_head_norm, dsa_qk_norm, w_branch, w_out, ffn2_norm, ffn2_w_gu, ffn2_w_down):
    b, s, d = x.shape
    depth = w_in.shape[0]
    t = b * s
    tl = _tiles(s)
    pos = positions.astype(F32)[..., None]
    perm, gate_off = _in_col_permutation(d)
    keep = jnp.asarray(perm >= 0)
    src = jnp.asarray(np.maximum(perm, 0))
    x2 = x.reshape(t, d)
    for l in range(depth):
        x2 = _ffn(x2, ffn1_norm[l][None, :], *_ffn_weights(ffn1_w_gu[l], ffn1_w_down[l], 256), tl["tm"])
        w_perm = jnp.where(keep[None, :], jnp.take(w_in[l], src, axis=1), 0.0).astype(BF16)
        z = _mixin(x2, mix_norm[l][None, :], w_perm, tl["tm"]).reshape(b, s, Z_COLS)
        outs = (
            _diff_attention(z, pos, diff_qk_norm[l], diff_lambda[l], diff_head_norm[l], l, tl["tq_diff"]),
            _mlstm(z, ml_conv_w[l], ml_conv_b[l], ml_gate_bias[l], ml_head_norm[l], tl["chunk"]),
            _stick_breaking(z, tl["tq_sb"]),
            _dsa(z, pos, dsa_qk_norm[l], tl["tq_dsa"]),
        )
        wgate = w_in[l][:, gate_off:].reshape(d, N_BRANCH, d).transpose(1, 0, 2).astype(BF16)
        x2 = _merge(x2, mix_norm[l][None, :], [o.reshape(t, BRANCH_W) for o in outs], wgate,
                    w_branch[l].astype(BF16), w_out[l].astype(BF16), tl["tm"])
        x2 = _ffn(x2, ffn2_norm[l][None, :], *_ffn_weights(ffn2_w_gu[l], ffn2_w_down[l], 256), tl["tm"])
    return x2.reshape(b, s, d)
```

```python
import functools
import math

import numpy as np
import jax
import jax.numpy as jnp
from jax import lax
from jax.experimental import pallas as pl
from jax.experimental.pallas import tpu as pltpu

F32 = jnp.float32
BF16 = jnp.bfloat16
I32 = jnp.int32

EPS = 1e-6
ROPE_THETA = 500000.0
ROPE_FRAC = 4
HEAD_DIM = 64
N_HEADS = 4
BRANCH_W = N_HEADS * HEAD_DIM
DIFF_DC = HEAD_DIM // 2
IDX_HEADS = 8
IDX_D = 32
DSA_TOPK_MAX = 256
CONV_W = 4
N_BRANCH = 4
NEG_BIG = -1e30
MASKED = -2e30
UNDERFLOW_LOG = -104.0
INT_MIN = -(2 ** 31)
LANE = 128
VMEM_LIMIT = 56 * 1024 * 1024

Z_DIFF_Q, Z_DIFF_K, Z_DIFF_V = 0, 256, 512
Z_ML_V, Z_ML_QK, Z_ML_O = 768, 1024, 1536
Z_SB_Q, Z_SB_K, Z_SB_V = 1792, 2048, 2304
Z_DSA_Q, Z_IDX_Q = 2560, 2816
Z_SM_A, Z_SM_B = 3072, 3200
Z_COLS = 3328
SM_GATE = 0
SM_IDXW = 4
SM_IDXK = 32
SM_KV = 64


def _in_col_permutation(d_model):
    src, off = {}, 0
    for name, w in (("diff_q", 256), ("diff_k", 256), ("diff_v", 256), ("ml_qk", 512), ("ml_v", 256),
                    ("ml_i", 4), ("ml_f", 4), ("ml_o", 256), ("sb_q", 256), ("sb_k", 256), ("sb_v", 256),
                    ("dsa_q", 256), ("dsa_k", 64), ("dsa_v", 64), ("idx_q", 256), ("idx_k", 32),
                    ("idx_w", 8), ("gates", N_BRANCH * d_model)):
        src[name] = (off, w)
        off += w
    perm = -np.ones((Z_COLS,), np.int64)

    def put(dst, name):
        o, w = src[name]
        perm[dst:dst + w] = np.arange(o, o + w)

    put(Z_DIFF_Q, "diff_q"); put(Z_DIFF_K, "diff_k"); put(Z_DIFF_V, "diff_v")
    put(Z_ML_QK, "ml_qk"); put(Z_ML_V, "ml_v"); put(Z_ML_O, "ml_o")
    put(Z_SB_Q, "sb_q"); put(Z_SB_K, "sb_k"); put(Z_SB_V, "sb_v")
    put(Z_DSA_Q, "dsa_q"); put(Z_IDX_Q, "idx_q")
    put(Z_SM_A + SM_GATE, "ml_i"); put(Z_SM_A + SM_IDXK, "idx_k"); put(Z_SM_A + SM_KV, "dsa_k")
    put(Z_SM_B + SM_GATE, "ml_f"); put(Z_SM_B + SM_IDXW, "idx_w"); put(Z_SM_B + SM_KV, "dsa_v")
    return perm, src["gates"][0]


def _rope_rows(width, group, rot_dim):
    half = rot_dim // 2
    inv = ROPE_THETA ** (-np.arange(half, dtype=np.float32) / half)
    r = np.arange(width) % group
    invf = np.where(r < rot_dim, inv[r % half], 0.0).astype(np.float32)
    m_lo = np.where(r < half, -1.0, 0.0).astype(np.float32)
    m_hi = np.where((r >= half) & (r < rot_dim), 1.0, 0.0).astype(np.float32)
    return jnp.asarray(np.stack([invf, m_lo, m_hi]))


def _group_mean_matrix(width, group):
    g = np.arange(width) // group
    return jnp.asarray((g[:, None] == g[None, :]).astype(np.float32) / group, dtype=BF16)


def _dot(a, b):
    return jnp.dot(a, b, preferred_element_type=F32)


def _dot_nt(a, b):
    return lax.dot_general(a, b, (((1,), (1,)), ((), ())), preferred_element_type=F32)


def _dot_split(a, b):
    hi = a.astype(BF16)
    lo = (a - hi.astype(F32)).astype(BF16)
    return _dot(hi, b) + _dot(lo, b)


def _rms(x, g):
    return x * lax.rsqrt(jnp.mean(x * x, axis=-1, keepdims=True) + EPS) * g


def _softplus(z):
    return jnp.maximum(z, 0.0) + jnp.log(1.0 + jnp.exp(-jnp.abs(z)))


def _group_norm(x, gmat, gain):
    ms = _dot_split(x * x, gmat)
    return x * lax.rsqrt(ms + EPS) * gain


def _rope(x, pos, rope_rows, half):
    w = x.shape[-1]
    ang = pos * rope_rows[0:1, :]
    cos, sin = jnp.cos(ang), jnp.sin(ang)
    partner = (pltpu.roll(x, w - half, 1) * rope_rows[1:2, :] + pltpu.roll(x, half, 1) * rope_rows[2:3, :])
    return x * cos + partner * sin


def _cparams(sem):
    return pltpu.CompilerParams(dimension_semantics=sem, vmem_limit_bytes=VMEM_LIMIT)


def _full(shape):
    n = len(shape)
    return pl.BlockSpec(shape, lambda *_: (0,) * n)


def _ffn_kernel(x_ref, g_ref, wg_ref, wu_ref, wd_ref, o_ref, h_ref, acc_ref, *, n_chunks):
    h_ref[...] = _rms(x_ref[...], g_ref[...]).astype(BF16)
    acc_ref[...] = jnp.zeros_like(acc_ref)

    def body(c, carry):
        h = h_ref[...]
        g = _dot(h, wg_ref[c])
        u = _dot(h, wu_ref[c])
        a = (g * jax.nn.sigmoid(g)) * u
        acc_ref[...] += _dot(a.astype(BF16), wd_ref[c])
        return carry

    lax.fori_loop(0, n_chunks, body, 0)
    o_ref[...] = x_ref[...] + 0.5 * acc_ref[...]


def _ffn(x2, g, wg3, wu3, wd3, tm):
    t, d = x2.shape
    n_chunks, _, tf = wg3.shape
    return pl.pallas_call(
        functools.partial(_ffn_kernel, n_chunks=n_chunks),
        grid=(t // tm,),
        in_specs=[pl.BlockSpec((tm, d), lambda i: (i, 0)), _full((1, d)),
                  _full(wg3.shape), _full(wu3.shape), _full(wd3.shape)],
        out_specs=pl.BlockSpec((tm, d), lambda i: (i, 0)),
        out_shape=jax.ShapeDtypeStruct((t, d), F32),
        scratch_shapes=[pltpu.VMEM((tm, d), BF16), pltpu.VMEM((tm, d), F32)],
        compiler_params=_cparams(("parallel",)),
        name="ffn",
    )(x2, g, wg3, wu3, wd3)


def _mixin_kernel(x_ref, g_ref, w_ref, z_ref, *, col_chunk):
    h = _rms(x_ref[...], g_ref[...]).astype(BF16)
    for c in range(0, z_ref.shape[-1], col_chunk):
        z_ref[:, c:c + col_chunk] = _dot(h, w_ref[:, c:c + col_chunk])


def _mixin(x2, g, w_perm, tm):
    t, d = x2.shape
    n = w_perm.shape[1]
    return pl.pallas_call(
        functools.partial(_mixin_kernel, col_chunk=256),
        grid=(t // tm,),
        in_specs=[pl.BlockSpec((tm, d), lambda i: (i, 0)), _full((1, d)), _full((d, n))],
        out_specs=pl.BlockSpec((tm, n), lambda i: (i, 0)),
        out_shape=jax.ShapeDtypeStruct((t, n), F32),
        compiler_params=_cparams(("parallel",)),
        name="mix_in",
    )(x2, g, w_perm)


def _merge_kernel(x_ref, g_ref, o0_ref, o1_ref, o2_ref, o3_ref, wgate_ref, wbr_ref, wout_ref, out_ref):
    x = x_ref[...]
    h = _rms(x, g_ref[...]).astype(BF16)
    y = None
    for b, o_ref in enumerate((o0_ref, o1_ref, o2_ref, o3_ref)):
        gate = jax.nn.sigmoid(_dot(h, wgate_ref[b]))
        yb = gate * _dot(o_ref[...], wbr_ref[b])
        y = yb if y is None else y + yb
    out_ref[...] = x + _dot(y.astype(BF16), wout_ref[...])


def _merge(x2, g, outs, wgate, wbr, wout, tm):
    t, d = x2.shape
    bw = wbr.shape[1]
    o_spec = pl.BlockSpec((tm, bw), lambda i: (i, 0))
    return pl.pallas_call(
        _merge_kernel,
        grid=(t // tm,),
        in_specs=[pl.BlockSpec((tm, d), lambda i: (i, 0)), _full((1, d)), o_spec, o_spec, o_spec, o_spec,
                  _full(wgate.shape), _full(wbr.shape), _full(wout.shape)],
        out_specs=pl.BlockSpec((tm, d), lambda i: (i, 0)),
        out_shape=jax.ShapeDtypeStruct((t, d), F32),
        compiler_params=_cparams(("parallel",)),
        name="merge",
    )(x2, g, *outs, wgate, wbr, wout)


def _diff_kernel(pos_ref, zq_ref, zk_ref, zv_ref, gq_ref, gk_ref, lp_ref, hg_ref, rope_ref, gmat_ref,
                 o_ref, k_scr, vt_scr, q_scr, s_scr, acc_scr, *, tq, tk, seq, lam_init):
    assert tq == tk
    qi = pl.program_id(1)
    half = DIFF_DC // ROPE_FRAC // 2
    rope_rows = rope_ref[...]
    gmat = gmat_ref[...]

    @pl.when(qi == 0)
    def _prep():
        def body(c, carry):
            rows = pl.ds(pl.multiple_of(c * tk, tk), tk)
            k = _group_norm(zk_ref[0, rows, :], gmat, gk_ref[...])
            k_scr[rows, :] = _rope(k, pos_ref[0, rows, :], rope_rows, half).astype(BF16)
            vt_scr[c] = zv_ref[0, rows, :].T.astype(BF16)
            return carry
        lax.fori_loop(0, seq // tk, body, 0)

    q0 = pl.multiple_of(qi * tq, tq)
    q = _group_norm(zq_ref[0], gmat, gq_ref[...])
    q = _rope(q, pos_ref[0, pl.ds(q0, tq), :], rope_rows, half) * (DIFF_DC ** -0.5)
    q_t = q.T
    n_maps = 2 * N_HEADS
    grp = lax.shift_right_logical(lax.broadcasted_iota(I32, q_t.shape, 0), DIFF_DC.bit_length() - 1)
    for g in range(n_maps):
        q_scr[:, g * tq:(g + 1) * tq] = jnp.where(grp == g, q_t, 0.0).astype(BF16)

    lp = lp_ref[...]
    lam = (jnp.exp(jnp.sum(lp[0:1] * lp[1:2], axis=-1, keepdims=True))
           - jnp.exp(jnp.sum(lp[2:3] * lp[3:4], axis=-1, keepdims=True)) + lam_init)

    acc_scr[...] = jnp.zeros_like(acc_scr)

    def block(j, carry, diagonal):
        m_all, l_all = carry
        kblk = k_scr[pl.ds(pl.multiple_of(j * tk, tk), tk), :]
        s_scr[...] = _dot(kblk, q_scr[...])
        vt = vt_scr[j]
        if diagonal:
            mask = lax.broadcasted_iota(I32, (tk, tq), 0) <= lax.broadcasted_iota(I32, (tk, tq), 1)
        m_out, l_out = [], []
        for h in range(N_HEADS):
            ps, alphas = [], []
            for c in range(2):
                g = 2 * h + c
                m_old = m_all[:, g * tq:(g + 1) * tq]
                s = s_scr[:, g * tq:(g + 1) * tq]
                if diagonal:
                    s = jnp.where(mask, s, MASKED)
                m_new = jnp.maximum(m_old, jnp.max(s, axis=0, keepdims=True))
                alpha = jnp.exp(m_old - m_new)
                p = jnp.exp(s - m_new)
                l_out.append(alpha * l_all[:, g * tq:(g + 1) * tq] + jnp.sum(p, axis=0, keepdims=True))
                m_out.append(m_new)
                ps.append(p.astype(BF16))
                alphas.append(alpha)
            cols = slice(2 * h * tq, (2 * h + 2) * tq)
            pv = _dot(vt[h * HEAD_DIM:(h + 1) * HEAD_DIM, :], jnp.concatenate(ps, axis=1))
            acc_scr[:, cols] = acc_scr[:, cols] * jnp.concatenate(alphas, axis=1) + pv
        return jnp.concatenate(m_out, axis=1), jnp.concatenate(l_out, axis=1)

    init = (jnp.full((1, n_maps * tq), NEG_BIG, F32), jnp.zeros((1, n_maps * tq), F32))
    carry = lax.fori_loop(0, qi, functools.partial(block, diagonal=False), init)
    _, l_all = block(qi, carry, diagonal=True)

    outs = []
    for h in range(N_HEADS):
        a = acc_scr[:, 2 * h * tq:(2 * h + 2) * tq] / l_all[:, 2 * h * tq:(2 * h + 2) * tq]
        o = a[:, :tq] - lam * a[:, tq:]
        o = o * lax.rsqrt(jnp.mean(o * o, axis=0, keepdims=True) + EPS) * hg_ref[...] * (1.0 - lam_init)
        outs.append(o)
    o_ref[0] = jnp.concatenate(outs, axis=0).T.astype(o_ref.dtype)


def _diff_attention(z, pos, qk_g, lam_p, head_g, layer_idx, tq, tk):
    b, s, _ = z.shape
    lam_init = 0.8 - 0.6 * math.exp(-0.3 * layer_idx)
    reps = BRANCH_W // DIFF_DC
    gq = jnp.tile(qk_g[0], reps)[None, :]
    gk = jnp.tile(qk_g[1], reps)[None, :]
    rope_rows = _rope_rows(BRANCH_W, DIFF_DC, DIFF_DC // ROPE_FRAC)
    gmat = _group_mean_matrix(BRANCH_W, DIFF_DC)
    blk = BRANCH_W
    return pl.pallas_call(
        functools.partial(_diff_kernel, tq=tq, tk=tk, seq=s, lam_init=lam_init),
        grid=(b, s // tq),
        in_specs=[pl.BlockSpec((1, s, 1), lambda bi, qi: (bi, 0, 0)),
                  pl.BlockSpec((1, tq, blk), lambda bi, qi: (bi, qi, Z_DIFF_Q // blk)),
                  pl.BlockSpec((1, s, blk), lambda bi, qi: (bi, 0, Z_DIFF_K // blk)),
                  pl.BlockSpec((1, s, blk), lambda bi, qi: (bi, 0, Z_DIFF_V // blk)),
                  _full((1, blk)), _full((1, blk)), _full(lam_p.shape), _full((HEAD_DIM, 1)),
                  _full(rope_rows.shape), _full(gmat.shape)],
        out_specs=pl.BlockSpec((1, tq, blk), lambda bi, qi: (bi, qi, 0)),
        out_shape=jax.ShapeDtypeStruct((b, s, blk), BF16),
        scratch_shapes=[pltpu.VMEM((s, blk), BF16), pltpu.VMEM((s // tk, blk, tk), BF16),
                        pltpu.VMEM((blk, 2 * N_HEADS * tq), BF16), pltpu.VMEM((tk, 2 * N_HEADS * tq), F32),
                        pltpu.VMEM((HEAD_DIM, 2 * N_HEADS * tq), F32)],
        compiler_params=_cparams(("parallel", "arbitrary")),
        name="diff_attn",
    )(pos, z, z, z, gq, gk, lam_p, head_g[:, None], rope_rows, gmat)


def _sb_kernel(zq_ref, zk_ref, zv_ref, o_ref, k_scr, vt_scr, q_scr, z_scr, acc_scr, *, tq, tk, seq):
    assert tq == tk
    qi = pl.program_id(1)

    @pl.when(qi == 0)
    def _prep():
        def body(c, carry):
            rows = pl.ds(pl.multiple_of(c * tk, tk), tk)
            k_scr[rows, :] = zk_ref[0, rows, :].astype(BF16)
            vt_scr[c] = zv_ref[0, rows, :].T.astype(BF16)
            return carry
        lax.fori_loop(0, seq // tk, body, 0)

    q_t = (zq_ref[0] * (HEAD_DIM ** -0.5)).T
    grp = lax.shift_right_logical(lax.broadcasted_iota(I32, q_t.shape, 0), HEAD_DIM.bit_length() - 1)
    for h in range(N_HEADS):
        q_scr[:, h * tq:(h + 1) * tq] = jnp.where(grp == h, q_t, 0.0).astype(BF16)
    later = (lax.broadcasted_iota(I32, (tk, tk), 1) > lax.broadcasted_iota(I32, (tk, tk), 0)).astype(BF16)
    acc_scr[...] = jnp.zeros_like(acc_scr)

    def block(j, tail_all, diagonal):
        kblk = k_scr[pl.ds(pl.multiple_of(j * tk, tk), tk), :]
        z_scr[...] = _dot(kblk, q_scr[...])
        vt = vt_scr[j]
        if diagonal:
            mask = lax.broadcasted_iota(I32, (tk, tq), 0) < lax.broadcasted_iota(I32, (tk, tq), 1)
        tails = []
        for h in range(N_HEADS):
            cols = slice(h * tq, (h + 1) * tq)
            z = z_scr[:, cols]
            if diagonal:
                z = jnp.where(mask, z, MASKED)
            sp = _softplus(z)
            hi = (-sp).astype(BF16)
            lo = (-sp - hi.astype(F32)).astype(BF16)
            both = _dot(later, jnp.concatenate([hi, lo], axis=1))
            inside = both[:, :tq] + both[:, tq:]
            a = jnp.exp(z - sp + inside + tail_all[:, cols])
            acc_scr[:, cols] += _dot(vt[h * HEAD_DIM:(h + 1) * HEAD_DIM, :], a.astype(BF16))
            tails.append(tail_all[:, cols] + inside[0:1, :] - sp[0:1, :])
        return jnp.concatenate(tails, axis=1)

    def alive_flag(tail_all):
        return (jnp.max(tail_all) > UNDERFLOW_LOG).astype(I32)

    def cond(carry):
        jj, alive, _ = carry
        return (jj < qi) & (alive > 0)

    def body(carry):
        jj, _, tail_all = carry
        tail_all = block(qi - 1 - jj, tail_all, diagonal=False)
        return jj + 1, alive_flag(tail_all), tail_all

    tail0 = block(qi, jnp.zeros((1, N_HEADS * tq), F32), diagonal=True)
    lax.while_loop(cond, body, (jnp.int32(0), alive_flag(tail0), tail0))
    o_ref[0] = jnp.concatenate([acc_scr[:, h * tq:(h + 1) * tq] for h in range(N_HEADS)], axis=0).T.astype(o_ref.dtype)


def _stick_breaking(z, tq, tk):
    b, s, _ = z.shape
    blk = BRANCH_W
    return pl.pallas_call(
        functools.partial(_sb_kernel, tq=tq, tk=tk, seq=s),
        grid=(b, s // tq),
        in_specs=[pl.BlockSpec((1, tq, blk), lambda bi, qi: (bi, qi, Z_SB_Q // blk)),
                  pl.BlockSpec((1, s, blk), lambda bi, qi: (bi, 0, Z_SB_K // blk)),
                  pl.BlockSpec((1, s, blk), lambda bi, qi: (bi, 0, Z_SB_V // blk))],
        out_specs=pl.BlockSpec((1, tq, blk), lambda bi, qi: (bi, qi, 0)),
        out_shape=jax.ShapeDtypeStruct((b, s, blk), BF16),
        scratch_shapes=[pltpu.VMEM((s, blk), BF16), pltpu.VMEM((s // tk, blk, tk), BF16),
                        pltpu.VMEM((blk, N_HEADS * tq), BF16), pltpu.VMEM((tk, N_HEADS * tq), F32),
                        pltpu.VMEM((HEAD_DIM, N_HEADS * tq), F32)],
        compiler_params=_cparams(("parallel", "arbitrary")),
        name="stick_breaking",
    )(z, z, z)


def _mlstm_kernel(zqk_ref, zv_ref, zo_ref, za_ref, zb_ref, cw_ref, cb_ref, bi_ref, bf_ref, hg_ref,
                  o_ref, xbuf, c_scr, m_scr, *, chunk):
    ci = pl.program_id(1)
    L = chunk
    pad = 8

    @pl.when(ci == 0)
    def _init():
        xbuf[0:pad, :] = jnp.zeros((pad, xbuf.shape[1]), F32)
        c_scr[...] = jnp.zeros_like(c_scr)
        m_scr[...] = jnp.zeros_like(m_scr)

    x = zqk_ref[0]
    xbuf[pad:pad + L, :] = x
    y = x * cw_ref[CONV_W - 1:CONV_W, :] + cb_ref[...]
    for d in range(1, CONV_W):
        y = y + xbuf[pad - d:pad - d + L, :] * cw_ref[CONV_W - 1 - d:CONV_W - d, :]
    xbuf[0:pad, :] = x[L - pad:L, :]
    qk = y * jax.nn.sigmoid(y)
    q_all = (qk[:, :BRANCH_W] * (HEAD_DIM ** -0.5)).astype(BF16)
    k_all = qk[:, BRANCH_W:]

    ig = za_ref[0] + bi_ref[...]
    fpre = zb_ref[0] + bf_ref[...]
    lf = -_softplus(-fpre)
    row = lax.broadcasted_iota(I32, (L, L), 0)
    col = lax.broadcasted_iota(I32, (L, L), 1)
    tri = row >= col
    bcum = _dot_split_left(tri.astype(BF16), lf)
    cmat = ig - bcum
    cmat_t = cmat.T

    v_all = zv_ref[0]
    lane = lax.broadcasted_iota(I32, (L, LANE), 1)
    ogate = jax.nn.sigmoid(zo_ref[0])

    for h in range(N_HEADS):
        lo = h * HEAD_DIM
        b_col = bcum[:, h:h + 1]
        c_row = cmat_t[h:h + 1, :]
        c_col = cmat[:, h:h + 1]
        m_old = m_scr[h:h + 1, 0:1]
        qh = q_all[:, lo:lo + HEAD_DIM]
        kh = k_all[:, lo:lo + HEAD_DIM]
        v2 = v_all[:, (h // 2) * LANE:(h // 2 + 1) * LANE]
        if h % 2 == 1:
            v2 = pltpu.roll(v2, HEAD_DIM, 1)
        v_aug = jnp.where(lane < HEAD_DIM, v2, (lane == HEAD_DIM).astype(F32)).astype(BF16)

        dl = jnp.where(tri, b_col + c_row, NEG_BIG)
        inter = b_col + m_old
        mt = jnp.maximum(inter, jnp.max(dl, axis=-1, keepdims=True))
        dw = jnp.exp(dl - mt)
        iw = jnp.exp(inter - mt)
        s = _dot_nt(qh, kh.astype(BF16)) * dw
        state = c_scr[h]
        qc = _dot(qh, state.astype(BF16))
        sv = _dot(s.astype(BF16), v_aug)
        num = iw * qc[:, :HEAD_DIM] + sv[:, :HEAD_DIM]
        den = iw * qc[:, HEAD_DIM:HEAD_DIM + 1] + jnp.sum(s, axis=-1, keepdims=True)
        hout = num / jnp.maximum(jnp.abs(den), jnp.exp(-mt))

        bl = b_col[L - 1:L, :]
        g = bl + c_col
        m_new = jnp.maximum(bl + m_old, jnp.max(g, axis=0, keepdims=True))
        decay = jnp.exp(bl + m_old - m_new)
        wk = jnp.exp(g - m_new)
        kw_t = _transpose_cols(kh * wk)
        c_scr[h] = decay * state + _dot(kw_t.astype(BF16), v_aug)
        m_scr[h:h + 1, :] = jnp.broadcast_to(m_new, (1, LANE))

        hn = _rms(hout, hg_ref[...])
        o_ref[0, :, lo:lo + HEAD_DIM] = (ogate[:, lo:lo + HEAD_DIM] * hn).astype(o_ref.dtype)


def _dot_split_left(a, b):
    hi = b.astype(BF16)
    lo = (b - hi.astype(F32)).astype(BF16)
    return _dot(a, hi) + _dot(a, lo)


def _transpose_cols(x):
    l, w = x.shape
    xp = jnp.concatenate([x, jnp.zeros((l, LANE - w), x.dtype)], axis=1)
    return xp.T[:w, :]


def _mlstm(z, conv_w, conv_b, gate_b, head_g, chunk):
    b, s, _ = z.shape
    bi = jnp.zeros((1, LANE), F32).at[0, SM_GATE:SM_GATE + N_HEADS].set(gate_b[0])
    bf = jnp.zeros((1, LANE), F32).at[0, SM_GATE:SM_GATE + N_HEADS].set(gate_b[1])
    wqk = 2 * BRANCH_W
    return pl.pallas_call(
        functools.partial(_mlstm_kernel, chunk=chunk),
        grid=(b, s // chunk),
        in_specs=[pl.BlockSpec((1, chunk, wqk), lambda bi_, ci: (bi_, ci, Z_ML_QK // wqk)),
                  pl.BlockSpec((1, chunk, BRANCH_W), lambda bi_, ci: (bi_, ci, Z_ML_V // BRANCH_W)),
                  pl.BlockSpec((1, chunk, BRANCH_W), lambda bi_, ci: (bi_, ci, Z_ML_O // BRANCH_W)),
                  pl.BlockSpec((1, chunk, LANE), lambda bi_, ci: (bi_, ci, Z_SM_A // LANE)),
                  pl.BlockSpec((1, chunk, LANE), lambda bi_, ci: (bi_, ci, Z_SM_B // LANE)),
                  _full(conv_w.shape), _full((1, wqk)), _full((1, LANE)), _full((1, LANE)), _full((1, HEAD_DIM))],
        out_specs=pl.BlockSpec((1, chunk, BRANCH_W), lambda bi_, ci: (bi_, ci, 0)),
        out_shape=jax.ShapeDtypeStruct((b, s, BRANCH_W), BF16),
        scratch_shapes=[pltpu.VMEM((chunk + 8, wqk), F32), pltpu.VMEM((N_HEADS, HEAD_DIM, LANE), F32),
                        pltpu.VMEM((8, LANE), F32)],
        compiler_params=_cparams(("parallel", "arbitrary")),
        name="mlstm",
    )(z, z, z, z, z, conv_w, conv_b[None, :], bi, bf, head_g[None, :])


def _dsa_key_rope_rows():
    lane = np.arange(LANE)
    rows = np.zeros((5, LANE), np.float32)
    for base, rot, mrow in ((SM_IDXK, IDX_D // ROPE_FRAC, 1), (SM_KV, HEAD_DIM // ROPE_FRAC, 3)):
        half = rot // 2
        inv = ROPE_THETA ** (-np.arange(half, dtype=np.float32) / half)
        r = lane - base
        inside = (r >= 0) & (r < rot)
        rows[0] = np.where(inside, inv[np.clip(r, 0, rot - 1) % half], rows[0])
        rows[mrow] = np.where((r >= 0) & (r < half), -1.0, 0.0)
        rows[mrow + 1] = np.where((r >= half) & (r < rot), 1.0, 0.0)
    return jnp.asarray(rows)


def _sortable_key(score):
    bits = lax.bitcast_convert_type(jnp.where(score == 0.0, 0.0, score), I32)
    return bits ^ (lax.shift_right_arithmetic(bits, 31) & 0x7FFFFFFF)


def _dsa_kernel(pos_ref, zq_ref, ziq_ref, za_ref, zb_ref, gq_ref, gk_ref, ropeq_ref, ropei_ref, ropek_ref,
                gmat_ref, o_ref, ka_scr, vt_scr, keys_scr, qi_scr, q_scr, r_scr, s_scr, acc_scr,
                *, tq, tk, seq, topk, idx_bits):
    qi = pl.program_id(1)
    n_blk = (qi * tq + tq + tk - 1) // tk
    half_q = HEAD_DIM // ROPE_FRAC // 2
    half_i = IDX_D // ROPE_FRAC // 2

    @pl.when(qi == 0)
    def _prep():
        is_k = lax.broadcasted_iota(I32, (tk, LANE), 1) >= SM_KV
        rk = ropek_ref[...]

        def body(c, carry):
            rows = pl.ds(pl.multiple_of(c * tk, tk), tk)
            a = za_ref[0, rows, :]
            ms = jnp.sum(jnp.where(is_k, a * a, 0.0), axis=-1, keepdims=True) * (1.0 / HEAD_DIM)
            a = jnp.where(is_k, a * lax.rsqrt(ms + EPS) * gk_ref[...], a)
            ang = pos_ref[0, rows, :] * rk[0:1, :]
            partner = (pltpu.roll(a, LANE - half_i, 1) * rk[1:2, :] + pltpu.roll(a, half_i, 1) * rk[2:3, :]
                       + pltpu.roll(a, LANE - half_q, 1) * rk[3:4, :] + pltpu.roll(a, half_q, 1) * rk[4:5, :])
            ka_scr[rows, :] = (a * jnp.cos(ang) + partner * jnp.sin(ang)).astype(BF16)
            vt_scr[c] = zb_ref[0, rows, :].T[SM_KV:SM_KV + HEAD_DIM, :].astype(BF16)
            return carry
        lax.fori_loop(0, seq // tk, body, 0)

    q0 = pl.multiple_of(qi * tq, tq)
    posq = pos_ref[0, pl.ds(q0, tq), :]

    q = _group_norm(zq_ref[0], gmat_ref[...], gq_ref[...])
    q_t = (_rope(q, posq, ropeq_ref[...], half_q) * (HEAD_DIM ** -0.5)).T
    q_scr[...] = jnp.zeros_like(q_scr)
    for h in range(N_HEADS):
        q_scr[SM_KV:SM_KV + HEAD_DIM, h * tq:(h + 1) * tq] = q_t[h * HEAD_DIM:(h + 1) * HEAD_DIM, :].astype(BF16)
    qx_t = _rope(ziq_ref[0], posq, ropei_ref[...], half_i).T
    qi_scr[...] = jnp.zeros_like(qi_scr)
    for h in range(IDX_HEADS):
        qi_scr[SM_IDXK:SM_IDXK + IDX_D, h * tq:(h + 1) * tq] = qx_t[h * IDX_D:(h + 1) * IDX_D, :].astype(BF16)
    w_t = zb_ref[0, pl.ds(q0, tq), :].T

    key_i = lax.broadcasted_iota(I32, (tk, tq), 0)
    qry_i = lax.broadcasted_iota(I32, (tk, tq), 1) + qi * tq

    def score_body(j, carry):
        kb = ka_scr[pl.ds(pl.multiple_of(j * tk, tk), tk), :]
        r_scr[...] = _dot(kb, qi_scr[...])
        score = None
        for h in range(IDX_HEADS):
            part = jnp.maximum(r_scr[:, h * tq:(h + 1) * tq], 0.0) * w_t[SM_IDXW + h:SM_IDXW + h + 1, :]
            score = part if score is None else score + part
        causal = (key_i + j * tk) <= qry_i
        keys_scr[j] = jnp.where(causal, _sortable_key(score), INT_MIN)
        return carry
    lax.fori_loop(0, n_blk, score_body, 0)

    def count(pred):
        def body(j, acc):
            ones = jnp.where(pred(keys_scr[j], j), 1.0, 0.0)
            return acc + jnp.sum(ones.reshape(tk // 8, 8, tq), axis=0)
        acc = lax.fori_loop(0, n_blk, body, jnp.zeros((8, tq), F32))
        return jnp.sum(acc, axis=0, keepdims=True)

    n_causal = qry_i[0:1, :] + 1
    settled0 = jnp.where(n_causal <= topk, 1.0, 0.0)

    def thr_cond(carry):
        it, pending, _, _ = carry
        return (it < 32) & (pending > 0)

    def thr_body(carry):
        it, _, t_u, settled = carry
        cand_u = t_u | lax.shift_left(jnp.int32(1), 31 - it)
        cand = cand_u ^ INT_MIN
        cnt = count(lambda kb, j: kb >= cand)
        t_u = jnp.where(cnt >= topk, cand_u, t_u)
        settled = jnp.where(cnt == topk, 1.0, settled)
        return it + 1, (jnp.min(settled) < 0.5).astype(I32), t_u, settled

    _, pending, t_u, _ = lax.while_loop(
        thr_cond, thr_body, (jnp.int32(0), (jnp.min(settled0) < 0.5).astype(I32), jnp.zeros((1, tq), I32), settled0))
    thr = t_u ^ INT_MIN
    has_thr = thr > INT_MIN

    @pl.when(pending > 0)
    def _break_ties():
        n_gt = count(lambda kb, j: kb > thr)
        n_ge = count(lambda kb, j: kb >= thr)
        need = topk - n_gt
        excess = has_thr & ((n_ge - n_gt) > need)

        def body(it, x):
            cand = x | lax.shift_left(jnp.int32(1), idx_bits - 1 - it)
            below = count(lambda kb, j: (kb == thr) & ((key_i + j * tk) < cand))
            return jnp.where(below < need, cand, x)
        last = lax.fori_loop(0, idx_bits, body, jnp.zeros((1, tq), I32))
        limit = jnp.where(excess, last, seq)

        def demote(j, carry):
            kb = keys_scr[j]
            keys_scr[j] = jnp.where((kb == thr) & ((key_i + j * tk) > limit), INT_MIN, kb)
            return carry
        lax.fori_loop(0, n_blk, demote, 0)

    thr_sel = jnp.where(has_thr, thr, INT_MIN + 1)
    acc_scr[...] = jnp.zeros_like(acc_scr)

    def att_body(j, carry):
        m_all, l_all = carry
        kb = ka_scr[pl.ds(pl.multiple_of(j * tk, tk), tk), :]
        s_scr[...] = _dot(kb, q_scr[...])
        sel = keys_scr[j] >= thr_sel
        vt = vt_scr[j]
        m_out, l_out = [], []
        for h in range(N_HEADS):
            cols = slice(h * tq, (h + 1) * tq)
            m_old = m_all[:, cols]
            s = jnp.where(sel, s_scr[:, cols], MASKED)
            m_new = jnp.maximum(m_old, jnp.max(s, axis=0, keepdims=True))
            alpha = jnp.exp(m_old - m_new)
            p = jnp.exp(s - m_new)
            l_out.append(alpha * l_all[:, cols] + jnp.sum(p, axis=0, keepdims=True))
            m_out.append(m_new)
            acc_scr[:, cols] = acc_scr[:, cols] * alpha + _dot(vt, p.astype(BF16))
        return jnp.concatenate(m_out, axis=1), jnp.concatenate(l_out, axis=1)

    init = (jnp.full((1, N_HEADS * tq), NEG_BIG, F32), jnp.zeros((1, N_HEADS * tq), F32))
    _, l_all = lax.fori_loop(0, n_blk, att_body, init)
    out = acc_scr[...] / l_all
    o_ref[0] = jnp.concatenate([out[:, h * tq:(h + 1) * tq] for h in range(N_HEADS)], axis=0).T.astype(o_ref.dtype)


def _dsa(z, pos, qk_g, tq, tk):
    b, s, _ = z.shape
    topk = min(DSA_TOPK_MAX, s // 4)
    gq = jnp.tile(qk_g[0], N_HEADS)[None, :]
    gk = jnp.zeros((1, LANE), F32).at[0, SM_KV:SM_KV + HEAD_DIM].set(qk_g[1])
    rope_q = _rope_rows(BRANCH_W, HEAD_DIM, HEAD_DIM // ROPE_FRAC)
    rope_i = _rope_rows(IDX_HEADS * IDX_D, IDX_D, IDX_D // ROPE_FRAC)
    rope_k = _dsa_key_rope_rows()
    gmat = _group_mean_matrix(BRANCH_W, HEAD_DIM)
    blk = BRANCH_W
    return pl.pallas_call(
        functools.partial(_dsa_kernel, tq=tq, tk=tk, seq=s, topk=float(topk),
                          idx_bits=max(1, (s - 1).bit_length())),
        grid=(b, s // tq),
        in_specs=[pl.BlockSpec((1, s, 1), lambda bi, qi: (bi, 0, 0)),
                  pl.BlockSpec((1, tq, blk), lambda bi, qi: (bi, qi, Z_DSA_Q // blk)),
                  pl.BlockSpec((1, tq, blk), lambda bi, qi: (bi, qi, Z_IDX_Q // blk)),
                  pl.BlockSpec((1, s, LANE), lambda bi, qi: (bi, 0, Z_SM_A // LANE)),
                  pl.BlockSpec((1, s, LANE), lambda bi, qi: (bi, 0, Z_SM_B // LANE)),
                  _full((1, blk)), _full((1, LANE)), _full(rope_q.shape), _full(rope_i.shape),
                  _full(rope_k.shape), _full(gmat.shape)],
        out_specs=pl.BlockSpec((1, tq, blk), lambda bi, qi: (bi, qi, 0)),
        out_shape=jax.ShapeDtypeStruct((b, s, blk), BF16),
        scratch_shapes=[pltpu.VMEM((s, LANE), BF16), pltpu.VMEM((s // tk, HEAD_DIM, tk), BF16),
                        pltpu.VMEM((s // tk, tk, tq), I32),
                        pltpu.VMEM((LANE, IDX_HEADS * tq), BF16), pltpu.VMEM((LANE, N_HEADS * tq), BF16),
                        pltpu.VMEM((tk, IDX_HEADS * tq), F32), pltpu.VMEM((tk, N_HEADS * tq), F32),
                        pltpu.VMEM((HEAD_DIM, N_HEADS * tq), F32)],
        compiler_params=_cparams(("parallel", "arbitrary")),
        name="dsa",
    )(pos, z, z, z, z, gq, gk, rope_q, rope_i, rope_k, gmat)


def _ffn_weights(w_gu, w_down, tf):
    d, two_f = w_gu.shape
    f = two_f // 2
    nc = f // tf
    wg3 = w_gu[:, :f].reshape(d, nc, tf).transpose(1, 0, 2).astype(BF16)
    wu3 = w_gu[:, f:].reshape(d, nc, tf).transpose(1, 0, 2).astype(BF16)
    wd3 = w_down.reshape(nc, tf, d).astype(BF16)
    return wg3, wu3, wd3


def _tiles(seq):
    return dict(tm=min(512, seq), tq=min(256, seq), tk=min(256, seq), chunk=min(256, seq))


def kernel(x, positions, ffn1_norm, ffn1_w_gu, ffn1_w_down, mix_norm, w_in, diff_qk_norm, diff_lambda, diff_head_norm, ml_conv_w, ml_conv_b, ml_gate_bias, ml_head_norm, dsa_qk_norm, w_branch, w_out, ffn2_norm, ffn2_w_gu, ffn2_w_down):
    b, s, d = x.shape
    depth = w_in.shape[0]
    t = b * s
    tl = _tiles(s)
    pos = positions.astype(F32)[..., None]
    perm, gate_off = _in_col_permutation(d)
    keep = jnp.asarray(perm >= 0)
    src = jnp.asarray(np.maximum(perm, 0))
    x2 = x.reshape(t, d)
    for l in range(depth):
        x2 = _ffn(x2, ffn1_norm[l][None, :], *_ffn_weights(ffn1_w_gu[l], ffn1_w_down[l], 256), tl["tm"])
        w_perm = jnp.where(keep[None, :], jnp.take(w_in[l], src, axis=1), 0.0).astype(BF16)
        z = _mixin(x2, mix_norm[l][None, :], w_perm, tl["tm"]).reshape(b, s, Z_COLS)
        outs = (
            _diff_attention(z, pos, diff_qk_norm[l], diff_lambda[l], diff_head_norm[l], l, tl["tq"], tl["tk"]),
            _mlstm(z, ml_conv_w[l], ml_conv_b[l], ml_gate_bias[l], ml_head_norm[l], tl["chunk"]),
            _stick_breaking(z, tl["tq"], tl["tk"]),
            _dsa(z, pos, dsa_qk_norm[l], tl["tq"], tl["tk"]),
        )
        wgate = w_in[l][:, gate_off:].reshape(d, N_BRANCH, d).transpose(1, 0, 2).astype(BF16)
        x2 = _merge(x2, mix_norm[l][None, :], [o.reshape(t, BRANCH_W) for o in outs], wgate,
                    w_branch[l].astype(BF16), w_out[l].astype(BF16), tl["tm"])
        x2 = _ffn(x2, ffn2_norm[l][None, :], *_ffn_weights(ffn2_w_gu[l], ffn2_w_down[l], 256), tl["tm"])
    return x2.reshape(b, s, d)
```

```python
import functools
import math

import numpy as np
import jax
import jax.numpy as jnp
from jax import lax
from jax.experimental import pallas as pl
from jax.experimental.pallas import tpu as pltpu

F32 = jnp.float32
BF16 = jnp.bfloat16
I32 = jnp.int32

EPS = 1e-6
ROPE_THETA = 500000.0
ROPE_FRAC = 4
HEAD_DIM = 64
N_HEADS = 4
BRANCH_W = N_HEADS * HEAD_DIM
DIFF_DC = HEAD_DIM // 2
IDX_HEADS = 8
IDX_D = 32
DSA_TOPK_MAX = 256
CONV_W = 4
N_BRANCH = 4
NEG_BIG = -1e30
MASKED = -2e30
UNDERFLOW_LOG = -104.0
INT_MIN = -(2 ** 31)
LANE = 128
VMEM_LIMIT = 56 * 1024 * 1024

Z_DIFF_Q, Z_DIFF_K, Z_DIFF_V = 0, 256, 512
Z_ML_V, Z_ML_QK, Z_ML_O = 768, 1024, 1536
Z_SB_Q, Z_SB_K, Z_SB_V = 1792, 2048, 2304
Z_DSA_Q, Z_IDX_Q = 2560, 2816
Z_SM_A, Z_SM_B = 3072, 3200
Z_COLS = 3328
SM_GATE = 0
SM_IDXW = 4
SM_IDXK = 32
SM_KV = 64


def _in_col_permutation(d_model):
    src, off = {}, 0
    for name, w in (("diff_q", 256), ("diff_k", 256), ("diff_v", 256), ("ml_qk", 512), ("ml_v", 256),
                    ("ml_i", 4), ("ml_f", 4), ("ml_o", 256), ("sb_q", 256), ("sb_k", 256), ("sb_v", 256),
                    ("dsa_q", 256), ("dsa_k", 64), ("dsa_v", 64), ("idx_q", 256), ("idx_k", 32),
                    ("idx_w", 8), ("gates", N_BRANCH * d_model)):
        src[name] = (off, w)
        off += w
    perm = -np.ones((Z_COLS,), np.int64)

    def put(dst, name):
        o, w = src[name]
        perm[dst:dst + w] = np.arange(o, o + w)

    put(Z_DIFF_Q, "diff_q"); put(Z_DIFF_K, "diff_k"); put(Z_DIFF_V, "diff_v")
    put(Z_ML_QK, "ml_qk"); put(Z_ML_V, "ml_v"); put(Z_ML_O, "ml_o")
    put(Z_SB_Q, "sb_q"); put(Z_SB_K, "sb_k"); put(Z_SB_V, "sb_v")
    put(Z_DSA_Q, "dsa_q"); put(Z_IDX_Q, "idx_q")
    put(Z_SM_A + SM_GATE, "ml_i"); put(Z_SM_A + SM_IDXK, "idx_k"); put(Z_SM_A + SM_KV, "dsa_k")
    put(Z_SM_B + SM_GATE, "ml_f"); put(Z_SM_B + SM_IDXW, "idx_w"); put(Z_SM_B + SM_KV, "dsa_v")
    return perm, src["gates"][0]


def _rope_rows(width, group, rot_dim):
    half = rot_dim // 2
    inv = ROPE_THETA ** (-np.arange(half, dtype=np.float32) / half)
    r = np.arange(width) % group
    invf = np.where(r < rot_dim, inv[r % half], 0.0).astype(np.float32)
    m_lo = np.where(r < half, -1.0, 0.0).astype(np.float32)
    m_hi = np.where((r >= half) & (r < rot_dim), 1.0, 0.0).astype(np.float32)
    return jnp.asarray(np.stack([invf, m_lo, m_hi]))


def _group_mean_matrix(width, group):
    g = np.arange(width) // group
    return jnp.asarray((g[:, None] == g[None, :]).astype(np.float32) / group, dtype=BF16)


def _dot(a, b):
    return jnp.dot(a, b, preferred_element_type=F32)


def _dot_nt(a, b):
    return lax.dot_general(a, b, (((1,), (1,)), ((), ())), preferred_element_type=F32)


def _dot_split(a, b):
    hi = a.astype(BF16)
    lo = (a - hi.astype(F32)).astype(BF16)
    return _dot(hi, b) + _dot(lo, b)


def _rms(x, g):
    return x * lax.rsqrt(jnp.mean(x * x, axis=-1, keepdims=True) + EPS) * g


def _softplus(z):
    return jnp.maximum(z, 0.0) + jnp.log(1.0 + jnp.exp(-jnp.abs(z)))


def _group_norm(x, gmat, gain):
    ms = _dot_split(x * x, gmat)
    return x * lax.rsqrt(ms + EPS) * gain


def _rope(x, pos, rope_rows, half):
    w = x.shape[-1]
    ang = pos * rope_rows[0:1, :]
    cos, sin = jnp.cos(ang), jnp.sin(ang)
    partner = (pltpu.roll(x, w - half, 1) * rope_rows[1:2, :] + pltpu.roll(x, half, 1) * rope_rows[2:3, :])
    return x * cos + partner * sin


def _cparams(sem):
    return pltpu.CompilerParams(dimension_semantics=sem, vmem_limit_bytes=VMEM_LIMIT)


def _full(shape):
    n = len(shape)
    return pl.BlockSpec(shape, lambda *_: (0,) * n)


def _ffn_kernel(x_ref, g_ref, wg_ref, wu_ref, wd_ref, o_ref, h_ref, a_ref, *, n_chunks, tf):
    h_ref[...] = _rms(x_ref[...], g_ref[...]).astype(BF16)
    for c in range(n_chunks):
        h = h_ref[...]
        g = _dot(h, wg_ref[c])
        u = _dot(h, wu_ref[c])
        a_ref[:, c * tf:(c + 1) * tf] = ((g * jax.nn.sigmoid(g)) * u).astype(BF16)
    o_ref[...] = x_ref[...] + 0.5 * _dot(a_ref[...], wd_ref[...])


def _ffn(x2, g, wg3, wu3, wd, tm):
    t, d = x2.shape
    n_chunks, _, tf = wg3.shape
    return pl.pallas_call(
        functools.partial(_ffn_kernel, n_chunks=n_chunks, tf=tf),
        grid=(t // tm,),
        in_specs=[pl.BlockSpec((tm, d), lambda i: (i, 0)), _full((1, d)),
                  _full(wg3.shape), _full(wu3.shape), _full(wd.shape)],
        out_specs=pl.BlockSpec((tm, d), lambda i: (i, 0)),
        out_shape=jax.ShapeDtypeStruct((t, d), F32),
        scratch_shapes=[pltpu.VMEM((tm, d), BF16), pltpu.VMEM((tm, n_chunks * tf), BF16)],
        compiler_params=_cparams(("parallel",)),
        name="ffn",
    )(x2, g, wg3, wu3, wd)


def _mixin_kernel(x_ref, g_ref, w_ref, z_ref, *, col_chunk):
    h = _rms(x_ref[...], g_ref[...]).astype(BF16)
    for c in range(0, z_ref.shape[-1], col_chunk):
        z_ref[:, c:c + col_chunk] = _dot(h, w_ref[:, c:c + col_chunk])


def _mixin(x2, g, w_perm, tm):
    t, d = x2.shape
    n = w_perm.shape[1]
    return pl.pallas_call(
        functools.partial(_mixin_kernel, col_chunk=256),
        grid=(t // tm,),
        in_specs=[pl.BlockSpec((tm, d), lambda i: (i, 0)), _full((1, d)), _full((d, n))],
        out_specs=pl.BlockSpec((tm, n), lambda i: (i, 0)),
        out_shape=jax.ShapeDtypeStruct((t, n), F32),
        compiler_params=_cparams(("parallel",)),
        name="mix_in",
    )(x2, g, w_perm)


def _merge_kernel(x_ref, g_ref, o0_ref, o1_ref, o2_ref, o3_ref, wgate_ref, wbr_ref, wout_ref, out_ref):
    x = x_ref[...]
    h = _rms(x, g_ref[...]).astype(BF16)
    y = None
    for b, o_ref in enumerate((o0_ref, o1_ref, o2_ref, o3_ref)):
        gate = jax.nn.sigmoid(_dot(h, wgate_ref[b]))
        yb = gate * _dot(o_ref[...], wbr_ref[b])
        y = yb if y is None else y + yb
    out_ref[...] = x + _dot(y.astype(BF16), wout_ref[...])


def _merge(x2, g, outs, wgate, wbr, wout, tm):
    t, d = x2.shape
    bw = wbr.shape[1]
    o_spec = pl.BlockSpec((tm, bw), lambda i: (i, 0))
    return pl.pallas_call(
        _merge_kernel,
        grid=(t // tm,),
        in_specs=[pl.BlockSpec((tm, d), lambda i: (i, 0)), _full((1, d)), o_spec, o_spec, o_spec, o_spec,
                  _full(wgate.shape), _full(wbr.shape), _full(wout.shape)],
        out_specs=pl.BlockSpec((tm, d), lambda i: (i, 0)),
        out_shape=jax.ShapeDtypeStruct((t, d), F32),
        compiler_params=_cparams(("parallel",)),
        name="merge",
    )(x2, g, *outs, wgate, wbr, wout)


def _diff_kernel(pos_ref, zq_ref, zk_ref, zv_ref, gq_ref, gk_ref, lp_ref, hg_ref, rope_ref, gmat_ref,
                 o_ref, k_scr, vt_scr, q_scr, s_scr, acc_scr, *, tq, tk, seq, lam_init):
    assert tq == tk
    qi = pl.program_id(1)
    half = DIFF_DC // ROPE_FRAC // 2
    rope_rows = rope_ref[...]
    gmat = gmat_ref[...]

    @pl.when(qi == 0)
    def _prep():
        def body(c, carry):
            rows = pl.ds(pl.multiple_of(c * tk, tk), tk)
            k = _group_norm(zk_ref[0, rows, :], gmat, gk_ref[...])
            k_scr[rows, :] = _rope(k, pos_ref[0, rows, :], rope_rows, half).astype(BF16)
            vt_scr[c] = zv_ref[0, rows, :].T.astype(BF16)
            return carry
        lax.fori_loop(0, seq // tk, body, 0)

    q0 = pl.multiple_of(qi * tq, tq)
    q = _group_norm(zq_ref[0], gmat, gq_ref[...])
    q = _rope(q, pos_ref[0, pl.ds(q0, tq), :], rope_rows, half) * (DIFF_DC ** -0.5)
    q_t = q.T
    n_maps = 2 * N_HEADS
    grp = lax.shift_right_logical(lax.broadcasted_iota(I32, q_t.shape, 0), DIFF_DC.bit_length() - 1)
    for g in range(n_maps):
        q_scr[:, g * tq:(g + 1) * tq] = jnp.where(grp == g, q_t, 0.0).astype(BF16)

    lp = lp_ref[...]
    lam = (jnp.exp(jnp.sum(lp[0:1] * lp[1:2], axis=-1, keepdims=True))
           - jnp.exp(jnp.sum(lp[2:3] * lp[3:4], axis=-1, keepdims=True)) + lam_init)

    acc_scr[...] = jnp.zeros_like(acc_scr)

    def block(j, carry, diagonal):
        m_all, l_all = carry
        kblk = k_scr[pl.ds(pl.multiple_of(j * tk, tk), tk), :]
        for g in range(n_maps):
            s_scr[:, g * tq:(g + 1) * tq] = _dot(kblk, q_scr[:, g * tq:(g + 1) * tq])
        vt = vt_scr[j]
        if diagonal:
            mask = lax.broadcasted_iota(I32, (tk, tq), 0) <= lax.broadcasted_iota(I32, (tk, tq), 1)
        m_out, l_out = [], []
        for h in range(N_HEADS):
            ps, alphas = [], []
            for c in range(2):
                g = 2 * h + c
                m_old = m_all[:, g * tq:(g + 1) * tq]
                s = s_scr[:, g * tq:(g + 1) * tq]
                if diagonal:
                    s = jnp.where(mask, s, MASKED)
                m_new = jnp.maximum(m_old, jnp.max(s, axis=0, keepdims=True))
                alpha = jnp.exp(m_old - m_new)
                p = jnp.exp(s - m_new)
                l_out.append(alpha * l_all[:, g * tq:(g + 1) * tq] + jnp.sum(p, axis=0, keepdims=True))
                m_out.append(m_new)
                ps.append(p.astype(BF16))
                alphas.append(alpha)
            cols = slice(2 * h * tq, (2 * h + 2) * tq)
            pv = _dot(vt[h * HEAD_DIM:(h + 1) * HEAD_DIM, :], jnp.concatenate(ps, axis=1))
            acc_scr[:, cols] = acc_scr[:, cols] * jnp.concatenate(alphas, axis=1) + pv
        return jnp.concatenate(m_out, axis=1), jnp.concatenate(l_out, axis=1)

    init = (jnp.full((1, n_maps * tq), NEG_BIG, F32), jnp.zeros((1, n_maps * tq), F32))
    carry = lax.fori_loop(0, qi, functools.partial(block, diagonal=False), init)
    _, l_all = block(qi, carry, diagonal=True)

    outs = []
    for h in range(N_HEADS):
        a = acc_scr[:, 2 * h * tq:(2 * h + 2) * tq] / l_all[:, 2 * h * tq:(2 * h + 2) * tq]
        o = a[:, :tq] - lam * a[:, tq:]
        o = o * lax.rsqrt(jnp.mean(o * o, axis=0, keepdims=True) + EPS) * hg_ref[...] * (1.0 - lam_init)
        outs.append(o)
    o_ref[0] = jnp.concatenate(outs, axis=0).T.astype(o_ref.dtype)


def _diff_attention(z, pos, qk_g, lam_p, head_g, layer_idx, tq, tk):
    b, s, _ = z.shape
    lam_init = 0.8 - 0.6 * math.exp(-0.3 * layer_idx)
    reps = BRANCH_W // DIFF_DC
    gq = jnp.tile(qk_g[0], reps)[None, :]
    gk = jnp.tile(qk_g[1], reps)[None, :]
    rope_rows = _rope_rows(BRANCH_W, DIFF_DC, DIFF_DC // ROPE_FRAC)
    gmat = _group_mean_matrix(BRANCH_W, DIFF_DC)
    blk = BRANCH_W
    return pl.pallas_call(
        functools.partial(_diff_kernel, tq=tq, tk=tk, seq=s, lam_init=lam_init),
        grid=(b, s // tq),
        in_specs=[pl.BlockSpec((1, s, 1), lambda bi, qi: (bi, 0, 0)),
                  pl.BlockSpec((1, tq, blk), lambda bi, qi: (bi, qi, Z_DIFF_Q // blk)),
                  pl.BlockSpec((1, s, blk), lambda bi, qi: (bi, 0, Z_DIFF_K // blk)),
                  pl.BlockSpec((1, s, blk), lambda bi, qi: (bi, 0, Z_DIFF_V // blk)),
                  _full((1, blk)), _full((1, blk)), _full(lam_p.shape), _full((HEAD_DIM, 1)),
                  _full(rope_rows.shape), _full(gmat.shape)],
        out_specs=pl.BlockSpec((1, tq, blk), lambda bi, qi: (bi, qi, 0)),
        out_shape=jax.ShapeDtypeStruct((b, s, blk), BF16),
        scratch_shapes=[pltpu.VMEM((s, blk), BF16), pltpu.VMEM((s // tk, blk, tk), BF16),
                        pltpu.VMEM((blk, 2 * N_HEADS * tq), BF16), pltpu.VMEM((tk, 2 * N_HEADS * tq), F32),
                        pltpu.VMEM((HEAD_DIM, 2 * N_HEADS * tq), F32)],
        compiler_params=_cparams(("parallel", "arbitrary")),
        name="diff_attn",
    )(pos, z, z, z, gq, gk, lam_p, head_g[:, None], rope_rows, gmat)


def _sb_kernel(zq_ref, zk_ref, zv_ref, o_ref, k_scr, vt_scr, q_scr, z_scr, acc_scr, *, tq, tk, seq):
    assert tq == tk
    qi = pl.program_id(1)

    @pl.when(qi == 0)
    def _prep():
        def body(c, carry):
            rows = pl.ds(pl.multiple_of(c * tk, tk), tk)
            k_scr[rows, :] = zk_ref[0, rows, :].astype(BF16)
            vt_scr[c] = zv_ref[0, rows, :].T.astype(BF16)
            return carry
        lax.fori_loop(0, seq // tk, body, 0)

    q_t = (zq_ref[0] * (HEAD_DIM ** -0.5)).T
    grp = lax.shift_right_logical(lax.broadcasted_iota(I32, q_t.shape, 0), HEAD_DIM.bit_length() - 1)
    for h in range(N_HEADS):
        q_scr[:, h * tq:(h + 1) * tq] = jnp.where(grp == h, q_t, 0.0).astype(BF16)
    later = (lax.broadcasted_iota(I32, (tk, tk), 1) > lax.broadcasted_iota(I32, (tk, tk), 0)).astype(BF16)
    acc_scr[...] = jnp.zeros_like(acc_scr)

    def block(j, tail_all, diagonal):
        kblk = k_scr[pl.ds(pl.multiple_of(j * tk, tk), tk), :]
        z_scr[...] = _dot(kblk, q_scr[...])
        vt = vt_scr[j]
        if diagonal:
            mask = lax.broadcasted_iota(I32, (tk, tq), 0) < lax.broadcasted_iota(I32, (tk, tq), 1)
        tails = []
        for h in range(N_HEADS):
            cols = slice(h * tq, (h + 1) * tq)
            z = z_scr[:, cols]
            if diagonal:
                z = jnp.where(mask, z, MASKED)
            sp = _softplus(z)
            hi = (-sp).astype(BF16)
            lo = (-sp - hi.astype(F32)).astype(BF16)
            both = _dot(later, jnp.concatenate([hi, lo], axis=1))
            inside = both[:, :tq] + both[:, tq:]
            a = jnp.exp(z - sp + inside + tail_all[:, cols])
            acc_scr[:, cols] += _dot(vt[h * HEAD_DIM:(h + 1) * HEAD_DIM, :], a.astype(BF16))
            tails.append(tail_all[:, cols] + inside[0:1, :] - sp[0:1, :])
        return jnp.concatenate(tails, axis=1)

    def alive_flag(tail_all):
        return (jnp.max(tail_all) > UNDERFLOW_LOG).astype(I32)

    def cond(carry):
        jj, alive, _ = carry
        return (jj < qi) & (alive > 0)

    def body(carry):
        jj, _, tail_all = carry
        tail_all = block(qi - 1 - jj, tail_all, diagonal=False)
        return jj + 1, alive_flag(tail_all), tail_all

    tail0 = block(qi, jnp.zeros((1, N_HEADS * tq), F32), diagonal=True)
    lax.while_loop(cond, body, (jnp.int32(0), alive_flag(tail0), tail0))
    o_ref[0] = jnp.concatenate([acc_scr[:, h * tq:(h + 1) * tq] for h in range(N_HEADS)], axis=0).T.astype(o_ref.dtype)


def _stick_breaking(z, tq, tk):
    b, s, _ = z.shape
    blk = BRANCH_W
    return pl.pallas_call(
        functools.partial(_sb_kernel, tq=tq, tk=tk, seq=s),
        grid=(b, s // tq),
        in_specs=[pl.BlockSpec((1, tq, blk), lambda bi, qi: (bi, qi, Z_SB_Q // blk)),
                  pl.BlockSpec((1, s, blk), lambda bi, qi: (bi, 0, Z_SB_K // blk)),
                  pl.BlockSpec((1, s, blk), lambda bi, qi: (bi, 0, Z_SB_V // blk))],
        out_specs=pl.BlockSpec((1, tq, blk), lambda bi, qi: (bi, qi, 0)),
        out_shape=jax.ShapeDtypeStruct((b, s, blk), BF16),
        scratch_shapes=[pltpu.VMEM((s, blk), BF16), pltpu.VMEM((s // tk, blk, tk), BF16),
                        pltpu.VMEM((blk, N_HEADS * tq), BF16), pltpu.VMEM((tk, N_HEADS * tq), F32),
                        pltpu.VMEM((HEAD_DIM, N_HEADS * tq), F32)],
        compiler_params=_cparams(("parallel", "arbitrary")),
        name="stick_breaking",
    )(z, z, z)


def _mlstm_kernel(zqk_ref, zv_ref, zo_ref, za_ref, zb_ref, cw_ref, cb_ref, bi_ref, bf_ref, hg_ref,
                  o_ref, xbuf, c_scr, m_scr, *, chunk):
    ci = pl.program_id(1)
    L = chunk
    pad = 8

    @pl.when(ci == 0)
    def _init():
        xbuf[0:pad, :] = jnp.zeros((pad, xbuf.shape[1]), F32)
        c_scr[...] = jnp.zeros_like(c_scr)
        m_scr[...] = jnp.zeros_like(m_scr)

    x = zqk_ref[0]
    xbuf[pad:pad + L, :] = x
    y = x * cw_ref[CONV_W - 1:CONV_W, :] + cb_ref[...]
    for d in range(1, CONV_W):
        y = y + xbuf[pad - d:pad - d + L, :] * cw_ref[CONV_W - 1 - d:CONV_W - d, :]
    xbuf[0:pad, :] = x[L - pad:L, :]
    qk = y * jax.nn.sigmoid(y)
    q_all = (qk[:, :BRANCH_W] * (HEAD_DIM ** -0.5)).astype(BF16)
    k_all = qk[:, BRANCH_W:]

    ig = za_ref[0] + bi_ref[...]
    fpre = zb_ref[0] + bf_ref[...]
    lf = -_softplus(-fpre)
    row = lax.broadcasted_iota(I32, (L, L), 0)
    col = lax.broadcasted_iota(I32, (L, L), 1)
    tri = row >= col
    bcum = _dot_split_left(tri.astype(BF16), lf)
    cmat = ig - bcum
    cmat_t = cmat.T

    v_all = zv_ref[0]
    lane = lax.broadcasted_iota(I32, (L, LANE), 1)
    ogate = jax.nn.sigmoid(zo_ref[0])

    for h in range(N_HEADS):
        lo = h * HEAD_DIM
        b_col = bcum[:, h:h + 1]
        c_row = cmat_t[h:h + 1, :]
        c_col = cmat[:, h:h + 1]
        m_old = m_scr[h:h + 1, 0:1]
        qh = q_all[:, lo:lo + HEAD_DIM]
        kh = k_all[:, lo:lo + HEAD_DIM]
        v2 = v_all[:, (h // 2) * LANE:(h // 2 + 1) * LANE]
        if h % 2 == 1:
            v2 = pltpu.roll(v2, HEAD_DIM, 1)
        v_aug = jnp.where(lane < HEAD_DIM, v2, (lane == HEAD_DIM).astype(F32)).astype(BF16)

        dl = jnp.where(tri, b_col + c_row, NEG_BIG)
        inter = b_col + m_old
        mt = jnp.maximum(inter, jnp.max(dl, axis=-1, keepdims=True))
        dw = jnp.exp(dl - mt)
        iw = jnp.exp(inter - mt)
        s = _dot_nt(qh, kh.astype(BF16)) * dw
        state = c_scr[h]
        qc = _dot(qh, state.astype(BF16))
        sv = _dot(s.astype(BF16), v_aug)
        num = iw * qc[:, :HEAD_DIM] + sv[:, :HEAD_DIM]
        den = iw * qc[:, HEAD_DIM:HEAD_DIM + 1] + jnp.sum(s, axis=-1, keepdims=True)
        hout = num / jnp.maximum(jnp.abs(den), jnp.exp(-mt))

        bl = b_col[L - 1:L, :]
        g = bl + c_col
        m_new = jnp.maximum(bl + m_old, jnp.max(g, axis=0, keepdims=True))
        decay = jnp.exp(bl + m_old - m_new)
        wk = jnp.exp(g - m_new)
        kw_t = _transpose_cols(kh * wk)
        c_scr[h] = decay * state + _dot(kw_t.astype(BF16), v_aug)
        m_scr[h:h + 1, :] = jnp.broadcast_to(m_new, (1, LANE))

        hn = _rms(hout, hg_ref[...])
        o_ref[0, :, lo:lo + HEAD_DIM] = (ogate[:, lo:lo + HEAD_DIM] * hn).astype(o_ref.dtype)


def _dot_split_left(a, b):
    hi = b.astype(BF16)
    lo = (b - hi.astype(F32)).astype(BF16)
    return _dot(a, hi) + _dot(a, lo)


def _transpose_cols(x):
    l, w = x.shape
    xp = jnp.concatenate([x, jnp.zeros((l, LANE - w), x.dtype)], axis=1)
    return xp.T[:w, :]


def _mlstm(z, conv_w, conv_b, gate_b, head_g, chunk):
    b, s, _ = z.shape
    bi = jnp.zeros((1, LANE), F32).at[0, SM_GATE:SM_GATE + N_HEADS].set(gate_b[0])
    bf = jnp.zeros((1, LANE), F32).at[0, SM_GATE:SM_GATE + N_HEADS].set(gate_b[1])
    wqk = 2 * BRANCH_W
    return pl.pallas_call(
        functools.partial(_mlstm_kernel, chunk=chunk),
        grid=(b, s // chunk),
        in_specs=[pl.BlockSpec((1, chunk, wqk), lambda bi_, ci: (bi_, ci, Z_ML_QK // wqk)),
                  pl.BlockSpec((1, chunk, BRANCH_W), lambda bi_, ci: (bi_, ci, Z_ML_V // BRANCH_W)),
                  pl.BlockSpec((1, chunk, BRANCH_W), lambda bi_, ci: (bi_, ci, Z_ML_O // BRANCH_W)),
                  pl.BlockSpec((1, chunk, LANE), lambda bi_, ci: (bi_, ci, Z_SM_A // LANE)),
                  pl.BlockSpec((1, chunk, LANE), lambda bi_, ci: (bi_, ci, Z_SM_B // LANE)),
                  _full(conv_w.shape), _full((1, wqk)), _full((1, LANE)), _full((1, LANE)), _full((1, HEAD_DIM))],
        out_specs=pl.BlockSpec((1, chunk, BRANCH_W), lambda bi_, ci: (bi_, ci, 0)),
        out_shape=jax.ShapeDtypeStruct((b, s, BRANCH_W), BF16),
        scratch_shapes=[pltpu.VMEM((chunk + 8, wqk), F32), pltpu.VMEM((N_HEADS, HEAD_DIM, LANE), F32),
                        pltpu.VMEM((8, LANE), F32)],
        compiler_params=_cparams(("parallel", "arbitrary")),
        name="mlstm",
    )(z, z, z, z, z, conv_w, conv_b[None, :], bi, bf, head_g[None, :])


def _dsa_key_rope_rows():
    lane = np.arange(LANE)
    rows = np.zeros((5, LANE), np.float32)
    for base, rot, mrow in ((SM_IDXK, IDX_D // ROPE_FRAC, 1), (SM_KV, HEAD_DIM // ROPE_FRAC, 3)):
        half = rot // 2
        inv = ROPE_THETA ** (-np.arange(half, dtype=np.float32) / half)
        r = lane - base
        inside = (r >= 0) & (r < rot)
        rows[0] = np.where(inside, inv[np.clip(r, 0, rot - 1) % half], rows[0])
        rows[mrow] = np.where((r >= 0) & (r < half), -1.0, 0.0)
        rows[mrow + 1] = np.where((r >= half) & (r < rot), 1.0, 0.0)
    return jnp.asarray(rows)


def _sortable_key(score):
    bits = lax.bitcast_convert_type(jnp.where(score == 0.0, 0.0, score), I32)
    return bits ^ (lax.shift_right_arithmetic(bits, 31) & 0x7FFFFFFF)


def _dsa_kernel(pos_ref, zq_ref, ziq_ref, za_ref, zb_ref, gq_ref, gk_ref, ropeq_ref, ropei_ref, ropek_ref,
                gmat_ref, o_ref, ka_scr, vt_scr, keys_scr, qi_scr, q_scr, r_scr, s_scr, acc_scr,
                *, tq, tk, seq, topk, idx_bits):
    qi = pl.program_id(1)
    n_blk = (qi * tq + tq + tk - 1) // tk
    half_q = HEAD_DIM // ROPE_FRAC // 2
    half_i = IDX_D // ROPE_FRAC // 2

    @pl.when(qi == 0)
    def _prep():
        is_k = lax.broadcasted_iota(I32, (tk, LANE), 1) >= SM_KV
        rk = ropek_ref[...]

        def body(c, carry):
            rows = pl.ds(pl.multiple_of(c * tk, tk), tk)
            a = za_ref[0, rows, :]
            ms = jnp.sum(jnp.where(is_k, a * a, 0.0), axis=-1, keepdims=True) * (1.0 / HEAD_DIM)
            a = jnp.where(is_k, a * lax.rsqrt(ms + EPS) * gk_ref[...], a)
            ang = pos_ref[0, rows, :] * rk[0:1, :]
            partner = (pltpu.roll(a, LANE - half_i, 1) * rk[1:2, :] + pltpu.roll(a, half_i, 1) * rk[2:3, :]
                       + pltpu.roll(a, LANE - half_q, 1) * rk[3:4, :] + pltpu.roll(a, half_q, 1) * rk[4:5, :])
            ka_scr[rows, :] = (a * jnp.cos(ang) + partner * jnp.sin(ang)).astype(BF16)
            vt_scr[c] = zb_ref[0, rows, :].T[SM_KV:SM_KV + HEAD_DIM, :].astype(BF16)
            return carry
        lax.fori_loop(0, seq // tk, body, 0)

    q0 = pl.multiple_of(qi * tq, tq)
    posq = pos_ref[0, pl.ds(q0, tq), :]

    q = _group_norm(zq_ref[0], gmat_ref[...], gq_ref[...])
    q_t = (_rope(q, posq, ropeq_ref[...], half_q) * (HEAD_DIM ** -0.5)).T
    q_scr[...] = jnp.zeros_like(q_scr)
    for h in range(N_HEADS):
        q_scr[SM_KV:SM_KV + HEAD_DIM, h * tq:(h + 1) * tq] = q_t[h * HEAD_DIM:(h + 1) * HEAD_DIM, :].astype(BF16)
    qx_t = _rope(ziq_ref[0], posq, ropei_ref[...], half_i).T
    qi_scr[...] = jnp.zeros_like(qi_scr)
    for h in range(IDX_HEADS):
        qi_scr[SM_IDXK:SM_IDXK + IDX_D, h * tq:(h + 1) * tq] = qx_t[h * IDX_D:(h + 1) * IDX_D, :].astype(BF16)
    w_t = zb_ref[0, pl.ds(q0, tq), :].T

    key_i = lax.broadcasted_iota(I32, (tk, tq), 0)
    qry_i = lax.broadcasted_iota(I32, (tk, tq), 1) + qi * tq

    def score_body(j, carry):
        kb = ka_scr[pl.ds(pl.multiple_of(j * tk, tk), tk), :]
        for h in range(IDX_HEADS):
            r_scr[:, h * tq:(h + 1) * tq] = _dot(kb, qi_scr[:, h * tq:(h + 1) * tq])
        score = None
        for h in range(IDX_HEADS):
            part = jnp.maximum(r_scr[:, h * tq:(h + 1) * tq], 0.0) * w_t[SM_IDXW + h:SM_IDXW + h + 1, :]
            score = part if score is None else score + part
        causal = (key_i + j * tk) <= qry_i
        keys_scr[j] = jnp.where(causal, _sortable_key(score), INT_MIN)
        return carry
    lax.fori_loop(0, n_blk, score_body, 0)

    def count(pred):
        def body(j, acc):
            ones = jnp.where(pred(keys_scr[j], j), 1.0, 0.0)
            return acc + jnp.sum(ones.reshape(tk // 8, 8, tq), axis=0)
        acc = lax.fori_loop(0, n_blk, body, jnp.zeros((8, tq), F32))
        return jnp.sum(acc, axis=0, keepdims=True)

    n_causal = qry_i[0:1, :] + 1
    settled0 = jnp.where(n_causal <= topk, 1.0, 0.0)

    def thr_cond(carry):
        it, pending, _, _ = carry
        return (it < 32) & (pending > 0)

    bits_per_check = 4

    def thr_body(carry):
        it, _, t_u, settled = carry
        for k in range(bits_per_check):
            cand_u = t_u | lax.shift_left(jnp.int32(1), 31 - k - it)
            cand = cand_u ^ INT_MIN
            cnt = count(lambda kb, j, cand=cand: kb >= cand)
            t_u = jnp.where(cnt >= topk, cand_u, t_u)
            settled = jnp.where(cnt == topk, 1.0, settled)
        return it + bits_per_check, (jnp.min(settled) < 0.5).astype(I32), t_u, settled

    _, pending, t_u, _ = lax.while_loop(
        thr_cond, thr_body, (jnp.int32(0), (jnp.min(settled0) < 0.5).astype(I32), jnp.zeros((1, tq), I32), settled0))
    thr = t_u ^ INT_MIN
    has_thr = thr > INT_MIN

    @pl.when(pending > 0)
    def _break_ties():
        n_gt = count(lambda kb, j: kb > thr)
        n_ge = count(lambda kb, j: kb >= thr)
        need = topk - n_gt
        excess = has_thr & ((n_ge - n_gt) > need)

        def body(it, x):
            cand = x | lax.shift_left(jnp.int32(1), idx_bits - 1 - it)
            below = count(lambda kb, j: (kb == thr) & ((key_i + j * tk) < cand))
            return jnp.where(below < need, cand, x)
        last = lax.fori_loop(0, idx_bits, body, jnp.zeros((1, tq), I32))
        limit = jnp.where(excess, last, seq)

        def demote(j, carry):
            kb = keys_scr[j]
            keys_scr[j] = jnp.where((kb == thr) & ((key_i + j * tk) > limit), INT_MIN, kb)
            return carry
        lax.fori_loop(0, n_blk, demote, 0)

    thr_sel = jnp.where(has_thr, thr, INT_MIN + 1)
    acc_scr[...] = jnp.zeros_like(acc_scr)

    def att_body(j, carry):
        m_all, l_all = carry
        kb = ka_scr[pl.ds(pl.multiple_of(j * tk, tk), tk), :]
        for h in range(N_HEADS):
            s_scr[:, h * tq:(h + 1) * tq] = _dot(kb, q_scr[:, h * tq:(h + 1) * tq])
        sel = keys_scr[j] >= thr_sel
        vt = vt_scr[j]
        m_out, l_out = [], []
        for h in range(N_HEADS):
            cols = slice(h * tq, (h + 1) * tq)
            m_old = m_all[:, cols]
            s = jnp.where(sel, s_scr[:, cols], MASKED)
            m_new = jnp.maximum(m_old, jnp.max(s, axis=0, keepdims=True))
            alpha = jnp.exp(m_old - m_new)
            p = jnp.exp(s - m_new)
            l_out.append(alpha * l_all[:, cols] + jnp.sum(p, axis=0, keepdims=True))
            m_out.append(m_new)
            acc_scr[:, cols] = acc_scr[:, cols] * alpha + _dot(vt, p.astype(BF16))
        return jnp.concatenate(m_out, axis=1), jnp.concatenate(l_out, axis=1)

    init = (jnp.full((1, N_HEADS * tq), NEG_BIG, F32), jnp.zeros((1, N_HEADS * tq), F32))
    _, l_all = lax.fori_loop(0, n_blk, att_body, init)
    out = acc_scr[...] / l_all
    o_ref[0] = jnp.concatenate([out[:, h * tq:(h + 1) * tq] for h in range(N_HEADS)], axis=0).T.astype(o_ref.dtype)


def _dsa(z, pos, qk_g, tq, tk):
    b, s, _ = z.shape
    topk = min(DSA_TOPK_MAX, s // 4)
    gq = jnp.tile(qk_g[0], N_HEADS)[None, :]
    gk = jnp.zeros((1, LANE), F32).at[0, SM_KV:SM_KV + HEAD_DIM].set(qk_g[1])
    rope_q = _rope_rows(BRANCH_W, HEAD_DIM, HEAD_DIM // ROPE_FRAC)
    rope_i = _rope_rows(IDX_HEADS * IDX_D, IDX_D, IDX_D // ROPE_FRAC)
    rope_k = _dsa_key_rope_rows()
    gmat = _group_mean_matrix(BRANCH_W, HEAD_DIM)
    blk = BRANCH_W
    return pl.pallas_call(
        functools.partial(_dsa_kernel, tq=tq, tk=tk, seq=s, topk=float(topk),
                          idx_bits=max(1, (s - 1).bit_length())),
        grid=(b, s // tq),
        in_specs=[pl.BlockSpec((1, s, 1), lambda bi, qi: (bi, 0, 0)),
                  pl.BlockSpec((1, tq, blk), lambda bi, qi: (bi, qi, Z_DSA_Q // blk)),
                  pl.BlockSpec((1, tq, blk), lambda bi, qi: (bi, qi, Z_IDX_Q // blk)),
                  pl.BlockSpec((1, s, LANE), lambda bi, qi: (bi, 0, Z_SM_A // LANE)),
                  pl.BlockSpec((1, s, LANE), lambda bi, qi: (bi, 0, Z_SM_B // LANE)),
                  _full((1, blk)), _full((1, LANE)), _full(rope_q.shape), _full(rope_i.shape),
                  _full(rope_k.shape), _full(gmat.shape)],
        out_specs=pl.BlockSpec((1, tq, blk), lambda bi, qi: (bi, qi, 0)),
        out_shape=jax.ShapeDtypeStruct((b, s, blk), BF16),
        scratch_shapes=[pltpu.VMEM((s, LANE), BF16), pltpu.VMEM((s // tk, HEAD_DIM, tk), BF16),
                        pltpu.VMEM((s // tk, tk, tq), I32),
                        pltpu.VMEM((LANE, IDX_HEADS * tq), BF16), pltpu.VMEM((LANE, N_HEADS * tq), BF16),
                        pltpu.VMEM((tk, IDX_HEADS * tq), F32), pltpu.VMEM((tk, N_HEADS * tq), F32),
                        pltpu.VMEM((HEAD_DIM, N_HEADS * tq), F32)],
        compiler_params=_cparams(("parallel", "arbitrary")),
        name="dsa",
    )(pos, z, z, z, z, gq, gk, rope_q, rope_i, rope_k, gmat)


def _ffn_weights(w_gu, w_down, tf):
    d, two_f = w_gu.shape
    f = two_f // 2
    nc = f // tf
    wg3 = w_gu[:, :f].reshape(d, nc, tf).transpose(1, 0, 2).astype(BF16)
    wu3 = w_gu[:, f:].reshape(d, nc, tf).transpose(1, 0, 2).astype(BF16)
    return wg3, wu3, w_down.astype(BF16)


def _tiles(seq):
    return dict(tm=min(512, seq), tq=min(256, seq), tk=min(256, seq), chunk=min(256, seq))


def kernel(x, positions, ffn1_norm, ffn1_w_gu, ffn1_w_down, mix_norm, w_in, diff_qk_norm, diff_lambda, diff_head_norm, ml_conv_w, ml_conv_b, ml_gate_bias, ml_head_norm, dsa_qk_norm, w_branch, w_out, ffn2_norm, ffn2_w_gu, ffn2_w_down):
    b, s, d = x.shape
    depth = w_in.shape[0]
    t = b * s
    tl = _tiles(s)
    pos = positions.astype(F32)[..., None]
    perm, gate_off = _in_col_permutation(d)
    keep = jnp.asarray(perm >= 0)
    src = jnp.asarray(np.maximum(perm, 0))
    x2 = x.reshape(t, d)
    for l in range(depth):
        x2 = _ffn(x2, ffn1_norm[l][None, :], *_ffn_weights(ffn1_w_gu[l], ffn1_w_down[l], 256), tl["tm"])
        w_perm = jnp.where(keep[None, :], jnp.take(w_in[l], src, axis=1), 0.0).astype(BF16)
        z = _mixin(x2, mix_norm[l][None, :], w_perm, tl["tm"]).reshape(b, s, Z_COLS)
        outs = (
            _diff_attention(z, pos, diff_qk_norm[l], diff_lambda[l], diff_head_norm[l], l, tl["tq"], tl["tk"]),
            _mlstm(z, ml_conv_w[l], ml_conv_b[l], ml_gate_bias[l], ml_head_norm[l], tl["chunk"]),
            _stick_breaking(z, tl["tq"], tl["tk"]),
            _dsa(z, pos, dsa_qk_norm[l], tl["tq"], tl["tk"]),
        )
        wgate = w_in[l][:, gate_off:].reshape(d, N_BRANCH, d).transpose(1, 0, 2).astype(BF16)
        x2 = _merge(x2, mix_norm[l][None, :], [o.reshape(t, BRANCH_W) for o in outs], wgate,
                    w_branch[l].astype(BF16), w_out[l].astype(BF16), tl["tm"])
        x2 = _ffn(x2, ffn2_norm[l][None, :], *_ffn_weights(ffn2_w_gu[l], ffn2_w_down[l], 256), tl["tm"])
    return x2.reshape(b, s, d)
```

```python
import functools
import math

import numpy as np
import jax
import jax.numpy as jnp
from jax import lax
from jax.experimental import pallas as pl
from jax.experimental.pallas import tpu as pltpu

F32 = jnp.float32
BF16 = jnp.bfloat16
I32 = jnp.int32

EPS = 1e-6
ROPE_THETA = 500000.0
ROPE_FRAC = 4
HEAD_DIM = 64
N_HEADS = 4
BRANCH_W = N_HEADS * HEAD_DIM
DIFF_DC = HEAD_DIM // 2
IDX_HEADS = 8
IDX_D = 32
DSA_TOPK_MAX = 256
CONV_W = 4
N_BRANCH = 4
NEG_BIG = -1e30
MASKED = -2e30
UNDERFLOW_LOG = -104.0
INT_MIN = -(2 ** 31)
LANE = 128
VMEM_LIMIT = 56 * 1024 * 1024

Z_DIFF_Q, Z_DIFF_K, Z_DIFF_V = 0, 256, 512
Z_ML_V, Z_ML_QK, Z_ML_O = 768, 1024, 1536
Z_SB_Q, Z_SB_K, Z_SB_V = 1792, 2048, 2304
Z_DSA_Q, Z_IDX_Q = 2560, 2816
Z_SM_A, Z_SM_B = 3072, 3200
Z_COLS = 3328
SM_GATE = 0
SM_IDXW = 4
SM_IDXK = 32
SM_KV = 64


def _in_col_permutation(d_model):
    src, off = {}, 0
    for name, w in (("diff_q", 256), ("diff_k", 256), ("diff_v", 256), ("ml_qk", 512), ("ml_v", 256),
                    ("ml_i", 4), ("ml_f", 4), ("ml_o", 256), ("sb_q", 256), ("sb_k", 256), ("sb_v", 256),
                    ("dsa_q", 256), ("dsa_k", 64), ("dsa_v", 64), ("idx_q", 256), ("idx_k", 32),
                    ("idx_w", 8), ("gates", N_BRANCH * d_model)):
        src[name] = (off, w)
        off += w
    perm = -np.ones((Z_COLS,), np.int64)

    def put(dst, name):
        o, w = src[name]
        perm[dst:dst + w] = np.arange(o, o + w)

    put(Z_DIFF_Q, "diff_q"); put(Z_DIFF_K, "diff_k"); put(Z_DIFF_V, "diff_v")
    put(Z_ML_QK, "ml_qk"); put(Z_ML_V, "ml_v"); put(Z_ML_O, "ml_o")
    put(Z_SB_Q, "sb_q"); put(Z_SB_K, "sb_k"); put(Z_SB_V, "sb_v")
    put(Z_DSA_Q, "dsa_q"); put(Z_IDX_Q, "idx_q")
    put(Z_SM_A + SM_GATE, "ml_i"); put(Z_SM_A + SM_IDXK, "idx_k"); put(Z_SM_A + SM_KV, "dsa_k")
    put(Z_SM_B + SM_GATE, "ml_f"); put(Z_SM_B + SM_IDXW, "idx_w"); put(Z_SM_B + SM_KV, "dsa_v")
    return perm, src["gates"][0]


def _rope_rows(width, group, rot_dim):
    half = rot_dim // 2
    inv = ROPE_THETA ** (-np.arange(half, dtype=np.float32) / half)
    r = np.arange(width) % group
    invf = np.where(r < rot_dim, inv[r % half], 0.0).astype(np.float32)
    m_lo = np.where(r < half, -1.0, 0.0).astype(np.float32)
    m_hi = np.where((r >= half) & (r < rot_dim), 1.0, 0.0).astype(np.float32)
    return jnp.asarray(np.stack([invf, m_lo, m_hi]))


def _group_mean_matrix(width, group):
    g = np.arange(width) // group
    return jnp.asarray((g[:, None] == g[None, :]).astype(np.float32) / group, dtype=BF16)


def _dot(a, b):
    return jnp.dot(a, b, preferred_element_type=F32)


def _dot_nt(a, b):
    return lax.dot_general(a, b, (((1,), (1,)), ((), ())), preferred_element_type=F32)


def _dot_split(a, b):
    hi = a.astype(BF16)
    lo = (a - hi.astype(F32)).astype(BF16)
    return _dot(hi, b) + _dot(lo, b)


def _rms(x, g):
    return x * lax.rsqrt(jnp.mean(x * x, axis=-1, keepdims=True) + EPS) * g


def _softplus(z):
    return jnp.maximum(z, 0.0) + jnp.log(1.0 + jnp.exp(-jnp.abs(z)))


def _group_norm(x, gmat, gain):
    ms = _dot_split(x * x, gmat)
    return x * lax.rsqrt(ms + EPS) * gain


def _rope_tables(pos, rope_rows):
    ang = pos * rope_rows[0:1, :LANE]
    return jnp.cos(ang), jnp.sin(ang)


def _rope(x, cos, sin, rope_rows, half):
    w = x.shape[-1]
    if w != LANE:
        cos = jnp.concatenate([cos] * (w // LANE), axis=1)
        sin = jnp.concatenate([sin] * (w // LANE), axis=1)
    partner = (pltpu.roll(x, w - half, 1) * rope_rows[1:2, :] + pltpu.roll(x, half, 1) * rope_rows[2:3, :])
    return x * cos + partner * sin


def _cparams(sem):
    return pltpu.CompilerParams(dimension_semantics=sem, vmem_limit_bytes=VMEM_LIMIT)


def _full(shape):
    n = len(shape)
    return pl.BlockSpec(shape, lambda *_: (0,) * n)


def _ffn_kernel(x_ref, g_ref, wgu_ref, wd_ref, o_ref, h_ref, a_ref, *, tf):
    d_ff = wd_ref.shape[0]
    h_ref[...] = _rms(x_ref[...], g_ref[...]).astype(BF16)
    for c in range(0, d_ff, tf):
        h = h_ref[...]
        g = _dot(h, wgu_ref[:, c:c + tf])
        u = _dot(h, wgu_ref[:, d_ff + c:d_ff + c + tf])
        a_ref[:, c:c + tf] = ((g * jax.nn.sigmoid(g)) * u).astype(BF16)
    o_ref[...] = x_ref[...] + 0.5 * _dot(a_ref[...], wd_ref[...])


def _ffn(x2, g, wgu, wd, tm, tf):
    t, d = x2.shape
    d_ff = wd.shape[0]
    assert d_ff % tf == 0 and wgu.shape == (d, 2 * d_ff)
    return pl.pallas_call(
        functools.partial(_ffn_kernel, tf=tf),
        grid=(t // tm,),
        in_specs=[pl.BlockSpec((tm, d), lambda i: (i, 0)), _full((1, d)), _full(wgu.shape), _full(wd.shape)],
        out_specs=pl.BlockSpec((tm, d), lambda i: (i, 0)),
        out_shape=jax.ShapeDtypeStruct((t, d), F32),
        scratch_shapes=[pltpu.VMEM((tm, d), BF16), pltpu.VMEM((tm, d_ff), BF16)],
        compiler_params=_cparams(("parallel",)),
        name="ffn",
    )(x2, g, wgu, wd)


def _mixin_kernel(x_ref, g_ref, w_ref, z_ref, *, col_chunk):
    h = _rms(x_ref[...], g_ref[...]).astype(BF16)
    for c in range(0, z_ref.shape[-1], col_chunk):
        z_ref[:, c:c + col_chunk] = _dot(h, w_ref[:, c:c + col_chunk])


def _mixin(x2, g, w_perm, tm):
    t, d = x2.shape
    n = w_perm.shape[1]
    return pl.pallas_call(
        functools.partial(_mixin_kernel, col_chunk=256),
        grid=(t // tm,),
        in_specs=[pl.BlockSpec((tm, d), lambda i: (i, 0)), _full((1, d)), _full((d, n))],
        out_specs=pl.BlockSpec((tm, n), lambda i: (i, 0)),
        out_shape=jax.ShapeDtypeStruct((t, n), F32),
        compiler_params=_cparams(("parallel",)),
        name="mix_in",
    )(x2, g, w_perm)


def _merge_kernel(x_ref, g_ref, o0_ref, o1_ref, o2_ref, o3_ref, wgate_ref, wbr_ref, wout_ref, out_ref):
    x = x_ref[...]
    h = _rms(x, g_ref[...]).astype(BF16)
    y = None
    for b, o_ref in enumerate((o0_ref, o1_ref, o2_ref, o3_ref)):
        gate = jax.nn.sigmoid(_dot(h, wgate_ref[b]))
        yb = gate * _dot(o_ref[...], wbr_ref[b])
        y = yb if y is None else y + yb
    out_ref[...] = x + _dot(y.astype(BF16), wout_ref[...])


def _merge(x2, g, outs, wgate, wbr, wout, tm):
    t, d = x2.shape
    bw = wbr.shape[1]
    o_spec = pl.BlockSpec((tm, bw), lambda i: (i, 0))
    return pl.pallas_call(
        _merge_kernel,
        grid=(t // tm,),
        in_specs=[pl.BlockSpec((tm, d), lambda i: (i, 0)), _full((1, d)), o_spec, o_spec, o_spec, o_spec,
                  _full(wgate.shape), _full(wbr.shape), _full(wout.shape)],
        out_specs=pl.BlockSpec((tm, d), lambda i: (i, 0)),
        out_shape=jax.ShapeDtypeStruct((t, d), F32),
        compiler_params=_cparams(("parallel",)),
        name="merge",
    )(x2, g, *outs, wgate, wbr, wout)


def _diff_kernel(pos_ref, zq_ref, zk_ref, zv_ref, gq_ref, gk_ref, lp_ref, hg_ref, rope_ref, gmat_ref,
                 o_ref, k_scr, vt_scr, q_scr, s_scr, acc_scr, cos_scr, sin_scr, *, tq, tk, seq, lam_init):
    assert tq == tk
    qi = pl.program_id(1)
    half = DIFF_DC // ROPE_FRAC // 2
    rope_rows = rope_ref[...]
    gmat = gmat_ref[...]

    @pl.when(qi == 0)
    def _prep():
        def body(c, carry):
            rows = pl.ds(pl.multiple_of(c * tk, tk), tk)
            k = _group_norm(zk_ref[0, rows, :], gmat, gk_ref[...])
            cos, sin = _rope_tables(pos_ref[0, rows, :], rope_rows)
            cos_scr[rows, :] = cos
            sin_scr[rows, :] = sin
            k_scr[rows, :] = _rope(k, cos, sin, rope_rows, half).astype(BF16)
            vt_scr[c] = zv_ref[0, rows, :].T.astype(BF16)
            return carry
        lax.fori_loop(0, seq // tk, body, 0)

    qrows = pl.ds(pl.multiple_of(qi * tq, tq), tq)
    q = _group_norm(zq_ref[0], gmat, gq_ref[...])
    q = _rope(q, cos_scr[qrows, :], sin_scr[qrows, :], rope_rows, half) * (DIFF_DC ** -0.5)
    q_t = q.T
    n_maps = 2 * N_HEADS
    grp = lax.shift_right_logical(lax.broadcasted_iota(I32, q_t.shape, 0), DIFF_DC.bit_length() - 1)
    for g in range(n_maps):
        q_scr[:, g * tq:(g + 1) * tq] = jnp.where(grp == g, q_t, 0.0).astype(BF16)

    lp = lp_ref[...]
    lam = (jnp.exp(jnp.sum(lp[0:1] * lp[1:2], axis=-1, keepdims=True))
           - jnp.exp(jnp.sum(lp[2:3] * lp[3:4], axis=-1, keepdims=True)) + lam_init)

    acc_scr[...] = jnp.zeros_like(acc_scr)

    def block(j, carry, diagonal):
        m_all, l_all = carry
        kblk = k_scr[pl.ds(pl.multiple_of(j * tk, tk), tk), :]
        for g in range(n_maps):
            s_scr[:, g * tq:(g + 1) * tq] = _dot(kblk, q_scr[:, g * tq:(g + 1) * tq])
        vt = vt_scr[j]
        if diagonal:
            mask = lax.broadcasted_iota(I32, (tk, tq), 0) <= lax.broadcasted_iota(I32, (tk, tq), 1)
        m_out, l_out = [], []
        for h in range(N_HEADS):
            ps, alphas = [], []
            for c in range(2):
                g = 2 * h + c
                m_old = m_all[:, g * tq:(g + 1) * tq]
                s = s_scr[:, g * tq:(g + 1) * tq]
                if diagonal:
                    s = jnp.where(mask, s, MASKED)
                m_new = jnp.maximum(m_old, jnp.max(s, axis=0, keepdims=True))
                alpha = jnp.exp(m_old - m_new)
                p = jnp.exp(s - m_new)
                l_out.append(alpha * l_all[:, g * tq:(g + 1) * tq] + jnp.sum(p, axis=0, keepdims=True))
                m_out.append(m_new)
                ps.append(p.astype(BF16))
                alphas.append(alpha)
            cols = slice(2 * h * tq, (2 * h + 2) * tq)
            pv = _dot(vt[h * HEAD_DIM:(h + 1) * HEAD_DIM, :], jnp.concatenate(ps, axis=1))
            acc_scr[:, cols] = acc_scr[:, cols] * jnp.concatenate(alphas, axis=1) + pv
        return jnp.concatenate(m_out, axis=1), jnp.concatenate(l_out, axis=1)

    init = (jnp.full((1, n_maps * tq), NEG_BIG, F32), jnp.zeros((1, n_maps * tq), F32))
    carry = lax.fori_loop(0, qi, functools.partial(block, diagonal=False), init)
    _, l_all = block(qi, carry, diagonal=True)

    outs = []
    for h in range(N_HEADS):
        a = acc_scr[:, 2 * h * tq:(2 * h + 2) * tq] / l_all[:, 2 * h * tq:(2 * h + 2) * tq]
        o = a[:, :tq] - lam * a[:, tq:]
        o = o * lax.rsqrt(jnp.mean(o * o, axis=0, keepdims=True) + EPS) * hg_ref[...] * (1.0 - lam_init)
        outs.append(o)
    o_ref[0] = jnp.concatenate(outs, axis=0).T.astype(o_ref.dtype)


def _diff_attention(z, pos, qk_g, lam_p, head_g, layer_idx, tq, tk):
    b, s, _ = z.shape
    lam_init = 0.8 - 0.6 * math.exp(-0.3 * layer_idx)
    reps = BRANCH_W // DIFF_DC
    gq = jnp.tile(qk_g[0], reps)[None, :]
    gk = jnp.tile(qk_g[1], reps)[None, :]
    rope_rows = _rope_rows(BRANCH_W, DIFF_DC, DIFF_DC // ROPE_FRAC)
    gmat = _group_mean_matrix(BRANCH_W, DIFF_DC)
    blk = BRANCH_W
    return pl.pallas_call(
        functools.partial(_diff_kernel, tq=tq, tk=tk, seq=s, lam_init=lam_init),
        grid=(b, s // tq),
        in_specs=[pl.BlockSpec((1, s, 1), lambda bi, qi: (bi, 0, 0)),
                  pl.BlockSpec((1, tq, blk), lambda bi, qi: (bi, qi, Z_DIFF_Q // blk)),
                  pl.BlockSpec((1, s, blk), lambda bi, qi: (bi, 0, Z_DIFF_K // blk)),
                  pl.BlockSpec((1, s, blk), lambda bi, qi: (bi, 0, Z_DIFF_V // blk)),
                  _full((1, blk)), _full((1, blk)), _full(lam_p.shape), _full((HEAD_DIM, 1)),
                  _full(rope_rows.shape), _full(gmat.shape)],
        out_specs=pl.BlockSpec((1, tq, blk), lambda bi, qi: (bi, qi, 0)),
        out_shape=jax.ShapeDtypeStruct((b, s, blk), BF16),
        scratch_shapes=[pltpu.VMEM((s, blk), BF16), pltpu.VMEM((s // tk, blk, tk), BF16),
                        pltpu.VMEM((blk, 2 * N_HEADS * tq), BF16), pltpu.VMEM((tk, 2 * N_HEADS * tq), F32),
                        pltpu.VMEM((HEAD_DIM, 2 * N_HEADS * tq), F32),
                        pltpu.VMEM((s, LANE), F32), pltpu.VMEM((s, LANE), F32)],
        compiler_params=_cparams(("parallel", "arbitrary")),
        name="diff_attn",
    )(pos, z, z, z, gq, gk, lam_p, head_g[:, None], rope_rows, gmat)


def _sb_kernel(zq_ref, zk_ref, zv_ref, o_ref, k_scr, vt_scr, q_scr, z_scr, acc_scr, *, tq, tk, seq):
    assert tq == tk
    qi = pl.program_id(1)

    @pl.when(qi == 0)
    def _prep():
        def body(c, carry):
            rows = pl.ds(pl.multiple_of(c * tk, tk), tk)
            k_scr[rows, :] = zk_ref[0, rows, :].astype(BF16)
            vt_scr[c] = zv_ref[0, rows, :].T.astype(BF16)
            return carry
        lax.fori_loop(0, seq // tk, body, 0)

    q_t = (zq_ref[0] * (HEAD_DIM ** -0.5)).T
    grp = lax.shift_right_logical(lax.broadcasted_iota(I32, q_t.shape, 0), HEAD_DIM.bit_length() - 1)
    for h in range(N_HEADS):
        q_scr[:, h * tq:(h + 1) * tq] = jnp.where(grp == h, q_t, 0.0).astype(BF16)
    later = (lax.broadcasted_iota(I32, (tk, tk), 1) > lax.broadcasted_iota(I32, (tk, tk), 0)).astype(BF16)
    acc_scr[...] = jnp.zeros_like(acc_scr)

    def block(j, tail_all, diagonal):
        kblk = k_scr[pl.ds(pl.multiple_of(j * tk, tk), tk), :]
        z_scr[...] = _dot(kblk, q_scr[...])
        vt = vt_scr[j]
        if diagonal:
            mask = lax.broadcasted_iota(I32, (tk, tq), 0) < lax.broadcasted_iota(I32, (tk, tq), 1)
        tails = []
        for h in range(N_HEADS):
            cols = slice(h * tq, (h + 1) * tq)
            z = z_scr[:, cols]
            if diagonal:
                z = jnp.where(mask, z, MASKED)
            sp = _softplus(z)
            hi = (-sp).astype(BF16)
            lo = (-sp - hi.astype(F32)).astype(BF16)
            both = _dot(later, jnp.concatenate([hi, lo], axis=1))
            inside = both[:, :tq] + both[:, tq:]
            a = jnp.exp(z - sp + inside + tail_all[:, cols])
            acc_scr[:, cols] += _dot(vt[h * HEAD_DIM:(h + 1) * HEAD_DIM, :], a.astype(BF16))
            tails.append(tail_all[:, cols] + inside[0:1, :] - sp[0:1, :])
        return jnp.concatenate(tails, axis=1)

    def alive_flag(tail_all):
        return (jnp.max(tail_all) > UNDERFLOW_LOG).astype(I32)

    def cond(carry):
        jj, alive, _ = carry
        return (jj < qi) & (alive > 0)

    def body(carry):
        jj, _, tail_all = carry
        tail_all = block(qi - 1 - jj, tail_all, diagonal=False)
        return jj + 1, alive_flag(tail_all), tail_all

    tail0 = block(qi, jnp.zeros((1, N_HEADS * tq), F32), diagonal=True)
    lax.while_loop(cond, body, (jnp.int32(0), alive_flag(tail0), tail0))
    o_ref[0] = jnp.concatenate([acc_scr[:, h * tq:(h + 1) * tq] for h in range(N_HEADS)], axis=0).T.astype(o_ref.dtype)


def _stick_breaking(z, tq, tk):
    b, s, _ = z.shape
    blk = BRANCH_W
    return pl.pallas_call(
        functools.partial(_sb_kernel, tq=tq, tk=tk, seq=s),
        grid=(b, s // tq),
        in_specs=[pl.BlockSpec((1, tq, blk), lambda bi, qi: (bi, qi, Z_SB_Q // blk)),
                  pl.BlockSpec((1, s, blk), lambda bi, qi: (bi, 0, Z_SB_K // blk)),
                  pl.BlockSpec((1, s, blk), lambda bi, qi: (bi, 0, Z_SB_V // blk))],
        out_specs=pl.BlockSpec((1, tq, blk), lambda bi, qi: (bi, qi, 0)),
        out_shape=jax.ShapeDtypeStruct((b, s, blk), BF16),
        scratch_shapes=[pltpu.VMEM((s, blk), BF16), pltpu.VMEM((s // tk, blk, tk), BF16),
                        pltpu.VMEM((blk, N_HEADS * tq), BF16), pltpu.VMEM((tk, N_HEADS * tq), F32),
                        pltpu.VMEM((HEAD_DIM, N_HEADS * tq), F32)],
        compiler_params=_cparams(("parallel", "arbitrary")),
        name="stick_breaking",
    )(z, z, z)


def _mlstm_kernel(zqk_ref, zv_ref, zo_ref, za_ref, zb_ref, cw_ref, cb_ref, bi_ref, bf_ref, hg_ref,
                  o_ref, xbuf, c_scr, m_scr, *, chunk):
    ci = pl.program_id(1)
    L = chunk
    pad = 8

    @pl.when(ci == 0)
    def _init():
        xbuf[0:pad, :] = jnp.zeros((pad, xbuf.shape[1]), F32)
        c_scr[...] = jnp.zeros_like(c_scr)
        m_scr[...] = jnp.zeros_like(m_scr)

    x = zqk_ref[0]
    xbuf[pad:pad + L, :] = x
    y = x * cw_ref[CONV_W - 1:CONV_W, :] + cb_ref[...]
    for d in range(1, CONV_W):
        y = y + xbuf[pad - d:pad - d + L, :] * cw_ref[CONV_W - 1 - d:CONV_W - d, :]
    xbuf[0:pad, :] = x[L - pad:L, :]
    qk = y * jax.nn.sigmoid(y)
    q_all = (qk[:, :BRANCH_W] * (HEAD_DIM ** -0.5)).astype(BF16)
    k_all = qk[:, BRANCH_W:]

    ig = za_ref[0] + bi_ref[...]
    fpre = zb_ref[0] + bf_ref[...]
    lf = -_softplus(-fpre)
    row = lax.broadcasted_iota(I32, (L, L), 0)
    col = lax.broadcasted_iota(I32, (L, L), 1)
    tri = row >= col
    bcum = _dot_split_left(tri.astype(BF16), lf)
    cmat = ig - bcum
    cmat_t = cmat.T

    v_all = zv_ref[0]
    lane = lax.broadcasted_iota(I32, (L, LANE), 1)
    ogate = jax.nn.sigmoid(zo_ref[0])

    for h in range(N_HEADS):
        lo = h * HEAD_DIM
        b_col = bcum[:, h:h + 1]
        c_row = cmat_t[h:h + 1, :]
        c_col = cmat[:, h:h + 1]
        m_old = m_scr[h:h + 1, 0:1]
        qh = q_all[:, lo:lo + HEAD_DIM]
        kh = k_all[:, lo:lo + HEAD_DIM]
        v2 = v_all[:, (h // 2) * LANE:(h // 2 + 1) * LANE]
        if h % 2 == 1:
            v2 = pltpu.roll(v2, HEAD_DIM, 1)
        v_aug = jnp.where(lane < HEAD_DIM, v2, (lane == HEAD_DIM).astype(F32)).astype(BF16)

        dl = jnp.where(tri, b_col + c_row, NEG_BIG)
        inter = b_col + m_old
        mt = jnp.maximum(inter, jnp.max(dl, axis=-1, keepdims=True))
        dw = jnp.exp(dl - mt)
        iw = jnp.exp(inter - mt)
        s = _dot_nt(qh, kh.astype(BF16)) * dw
        state = c_scr[h]
        qc = _dot(qh, state.astype(BF16))
        sv = _dot(s.astype(BF16), v_aug)
        num = iw * qc[:, :HEAD_DIM] + sv[:, :HEAD_DIM]
        den = iw * qc[:, HEAD_DIM:HEAD_DIM + 1] + jnp.sum(s, axis=-1, keepdims=True)
        hout = num / jnp.maximum(jnp.abs(den), jnp.exp(-mt))

        bl = b_col[L - 1:L, :]
        g = bl + c_col
        m_new = jnp.maximum(bl + m_old, jnp.max(g, axis=0, keepdims=True))
        decay = jnp.exp(bl + m_old - m_new)
        wk = jnp.exp(g - m_new)
        kw_t = _transpose_cols(kh * wk)
        c_scr[h] = decay * state + _dot(kw_t.astype(BF16), v_aug)
        m_scr[h:h + 1, :] = jnp.broadcast_to(m_new, (1, LANE))

        hn = _rms(hout, hg_ref[...])
        o_ref[0, :, lo:lo + HEAD_DIM] = (ogate[:, lo:lo + HEAD_DIM] * hn).astype(o_ref.dtype)


def _dot_split_left(a, b):
    hi = b.astype(BF16)
    lo = (b - hi.astype(F32)).astype(BF16)
    return _dot(a, hi) + _dot(a, lo)


def _transpose_cols(x):
    l, w = x.shape
    xp = jnp.concatenate([x, jnp.zeros((l, LANE - w), x.dtype)], axis=1)
    return xp.T[:w, :]


def _mlstm(z, conv_w, conv_b, gate_b, head_g, chunk):
    b, s, _ = z.shape
    bi = jnp.zeros((1, LANE), F32).at[0, SM_GATE:SM_GATE + N_HEADS].set(gate_b[0])
    bf = jnp.zeros((1, LANE), F32).at[0, SM_GATE:SM_GATE + N_HEADS].set(gate_b[1])
    wqk = 2 * BRANCH_W
    return pl.pallas_call(
        functools.partial(_mlstm_kernel, chunk=chunk),
        grid=(b, s // chunk),
        in_specs=[pl.BlockSpec((1, chunk, wqk), lambda bi_, ci: (bi_, ci, Z_ML_QK // wqk)),
                  pl.BlockSpec((1, chunk, BRANCH_W), lambda bi_, ci: (bi_, ci, Z_ML_V // BRANCH_W)),
                  pl.BlockSpec((1, chunk, BRANCH_W), lambda bi_, ci: (bi_, ci, Z_ML_O // BRANCH_W)),
                  pl.BlockSpec((1, chunk, LANE), lambda bi_, ci: (bi_, ci, Z_SM_A // LANE)),
                  pl.BlockSpec((1, chunk, LANE), lambda bi_, ci: (bi_, ci, Z_SM_B // LANE)),
                  _full(conv_w.shape), _full((1, wqk)), _full((1, LANE)), _full((1, LANE)), _full((1, HEAD_DIM))],
        out_specs=pl.BlockSpec((1, chunk, BRANCH_W), lambda bi_, ci: (bi_, ci, 0)),
        out_shape=jax.ShapeDtypeStruct((b, s, BRANCH_W), BF16),
        scratch_shapes=[pltpu.VMEM((chunk + 8, wqk), F32), pltpu.VMEM((N_HEADS, HEAD_DIM, LANE), F32),
                        pltpu.VMEM((8, LANE), F32)],
        compiler_params=_cparams(("parallel", "arbitrary")),
        name="mlstm",
    )(z, z, z, z, z, conv_w, conv_b[None, :], bi, bf, head_g[None, :])


def _dsa_key_rope_rows():
    lane = np.arange(LANE)
    rows = np.zeros((5, LANE), np.float32)
    for base, rot, mrow in ((SM_IDXK, IDX_D // ROPE_FRAC, 1), (SM_KV, HEAD_DIM // ROPE_FRAC, 3)):
        half = rot // 2
        inv = ROPE_THETA ** (-np.arange(half, dtype=np.float32) / half)
        r = lane - base
        inside = (r >= 0) & (r < rot)
        rows[0] = np.where(inside, inv[np.clip(r, 0, rot - 1) % half], rows[0])
        rows[mrow] = np.where((r >= 0) & (r < half), -1.0, 0.0)
        rows[mrow + 1] = np.where((r >= half) & (r < rot), 1.0, 0.0)
    return jnp.asarray(rows)


def _sortable_key(score):
    bits = lax.bitcast_convert_type(jnp.where(score == 0.0, 0.0, score), I32)
    return bits ^ (lax.shift_right_arithmetic(bits, 31) & 0x7FFFFFFF)


def _dsa_kernel(pos_ref, zq_ref, ziq_ref, za_ref, zb_ref, gq_ref, gk_ref, ropeq_ref, ropei_ref, ropek_ref,
                gmat_ref, o_ref, ka_scr, vt_scr, keys_scr, qi_scr, q_scr, r_scr, s_scr, acc_scr,
                cq_scr, sq_scr, ci_scr, si_scr, *, tq, tk, seq, topk, idx_bits):
    qi = pl.program_id(1)
    n_blk = (qi * tq + tq + tk - 1) // tk
    half_q = HEAD_DIM // ROPE_FRAC // 2
    half_i = IDX_D // ROPE_FRAC // 2

    @pl.when(qi == 0)
    def _prep():
        is_k = lax.broadcasted_iota(I32, (tk, LANE), 1) >= SM_KV
        rk = ropek_ref[...]

        def body(c, carry):
            rows = pl.ds(pl.multiple_of(c * tk, tk), tk)
            a = za_ref[0, rows, :]
            ms = jnp.sum(jnp.where(is_k, a * a, 0.0), axis=-1, keepdims=True) * (1.0 / HEAD_DIM)
            a = jnp.where(is_k, a * lax.rsqrt(ms + EPS) * gk_ref[...], a)
            ang = pos_ref[0, rows, :] * rk[0:1, :]
            partner = (pltpu.roll(a, LANE - half_i, 1) * rk[1:2, :] + pltpu.roll(a, half_i, 1) * rk[2:3, :]
                       + pltpu.roll(a, LANE - half_q, 1) * rk[3:4, :] + pltpu.roll(a, half_q, 1) * rk[4:5, :])
            ka_scr[rows, :] = (a * jnp.cos(ang) + partner * jnp.sin(ang)).astype(BF16)
            vt_scr[c] = zb_ref[0, rows, :].T[SM_KV:SM_KV + HEAD_DIM, :].astype(BF16)
            cq_scr[rows, :], sq_scr[rows, :] = _rope_tables(pos_ref[0, rows, :], ropeq_ref[...])
            ci_scr[rows, :], si_scr[rows, :] = _rope_tables(pos_ref[0, rows, :], ropei_ref[...])
            return carry
        lax.fori_loop(0, seq // tk, body, 0)

    q0 = pl.multiple_of(qi * tq, tq)
    qrows = pl.ds(q0, tq)

    q = _group_norm(zq_ref[0], gmat_ref[...], gq_ref[...])
    q_t = (_rope(q, cq_scr[qrows, :], sq_scr[qrows, :], ropeq_ref[...], half_q) * (HEAD_DIM ** -0.5)).T
    q_scr[...] = jnp.zeros_like(q_scr)
    for h in range(N_HEADS):
        q_scr[SM_KV:SM_KV + HEAD_DIM, h * tq:(h + 1) * tq] = q_t[h * HEAD_DIM:(h + 1) * HEAD_DIM, :].astype(BF16)
    qx_t = _rope(ziq_ref[0], ci_scr[qrows, :], si_scr[qrows, :], ropei_ref[...], half_i).T
    qi_scr[...] = jnp.zeros_like(qi_scr)
    for h in range(IDX_HEADS):
        qi_scr[SM_IDXK:SM_IDXK + IDX_D, h * tq:(h + 1) * tq] = qx_t[h * IDX_D:(h + 1) * IDX_D, :].astype(BF16)
    w_t = zb_ref[0, pl.ds(q0, tq), :].T

    key_i = lax.broadcasted_iota(I32, (tk, tq), 0)
    qry_i = lax.broadcasted_iota(I32, (tk, tq), 1) + qi * tq

    def score_body(j, carry):
        kb = ka_scr[pl.ds(pl.multiple_of(j * tk, tk), tk), :]
        for h in range(IDX_HEADS):
            r_scr[:, h * tq:(h + 1) * tq] = _dot(kb, qi_scr[:, h * tq:(h + 1) * tq])
        score = None
        for h in range(IDX_HEADS):
            part = jnp.maximum(r_scr[:, h * tq:(h + 1) * tq], 0.0) * w_t[SM_IDXW + h:SM_IDXW + h + 1, :]
            score = part if score is None else score + part
        causal = (key_i + j * tk) <= qry_i
        keys_scr[j] = jnp.where(causal, _sortable_key(score), INT_MIN)
        return carry
    lax.fori_loop(0, n_blk, score_body, 0)

    def count(pred):
        def body(j, acc):
            ones = jnp.where(pred(keys_scr[j], j), 1.0, 0.0)
            return acc + jnp.sum(ones.reshape(tk // 8, 8, tq), axis=0)
        acc = lax.fori_loop(0, n_blk, body, jnp.zeros((8, tq), F32))
        return jnp.sum(acc, axis=0, keepdims=True)

    n_causal = qry_i[0:1, :] + 1
    settled0 = jnp.where(n_causal <= topk, 1.0, 0.0)

    def thr_cond(carry):
        it, pending, _, _ = carry
        return (it < 32) & (pending > 0)

    bits_per_check = 4

    def thr_body(carry):
        it, _, t_u, settled = carry
        for k in range(bits_per_check):
            cand_u = t_u | lax.shift_left(jnp.int32(1), 31 - k - it)
            cand = cand_u ^ INT_MIN
            cnt = count(lambda kb, j, cand=cand: kb >= cand)
            t_u = jnp.where(cnt >= topk, cand_u, t_u)
            settled = jnp.where(cnt == topk, 1.0, settled)
        return it + bits_per_check, (jnp.min(settled) < 0.5).astype(I32), t_u, settled

    _, pending, t_u, _ = lax.while_loop(
        thr_cond, thr_body, (jnp.int32(0), (jnp.min(settled0) < 0.5).astype(I32), jnp.zeros((1, tq), I32), settled0))
    thr = t_u ^ INT_MIN
    has_thr = thr > INT_MIN

    @pl.when(pending > 0)
    def _break_ties():
        n_gt = count(lambda kb, j: kb > thr)
        n_ge = count(lambda kb, j: kb >= thr)
        need = topk - n_gt
        excess = has_thr & ((n_ge - n_gt) > need)

        def body(it, x):
            cand = x | lax.shift_left(jnp.int32(1), idx_bits - 1 - it)
            below = count(lambda kb, j: (kb == thr) & ((key_i + j * tk) < cand))
            return jnp.where(below < need, cand, x)
        last = lax.fori_loop(0, idx_bits, body, jnp.zeros((1, tq), I32))
        limit = jnp.where(excess, last, seq)

        def demote(j, carry):
            kb = keys_scr[j]
            keys_scr[j] = jnp.where((kb == thr) & ((key_i + j * tk) > limit), INT_MIN, kb)
            return carry
        lax.fori_loop(0, n_blk, demote, 0)

    thr_sel = jnp.where(has_thr, thr, INT_MIN + 1)
    acc_scr[...] = jnp.zeros_like(acc_scr)

    def att_body(j, carry):
        m_all, l_all = carry
        kb = ka_scr[pl.ds(pl.multiple_of(j * tk, tk), tk), :]
        for h in range(N_HEADS):
            s_scr[:, h * tq:(h + 1) * tq] = _dot(kb, q_scr[:, h * tq:(h + 1) * tq])
        sel = keys_scr[j] >= thr_sel
        vt = vt_scr[j]
        m_out, l_out = [], []
        for h in range(N_HEADS):
            cols = slice(h * tq, (h + 1) * tq)
            m_old = m_all[:, cols]
            s = jnp.where(sel, s_scr[:, cols], MASKED)
            m_new = jnp.maximum(m_old, jnp.max(s, axis=0, keepdims=True))
            alpha = jnp.exp(m_old - m_new)
            p = jnp.exp(s - m_new)
            l_out.append(alpha * l_all[:, cols] + jnp.sum(p, axis=0, keepdims=True))
            m_out.append(m_new)
            acc_scr[:, cols] = acc_scr[:, cols] * alpha + _dot(vt, p.astype(BF16))
        return jnp.concatenate(m_out, axis=1), jnp.concatenate(l_out, axis=1)

    init = (jnp.full((1, N_HEADS * tq), NEG_BIG, F32), jnp.zeros((1, N_HEADS * tq), F32))
    _, l_all = lax.fori_loop(0, n_blk, att_body, init)
    out = acc_scr[...] / l_all
    o_ref[0] = jnp.concatenate([out[:, h * tq:(h + 1) * tq] for h in range(N_HEADS)], axis=0).T.astype(o_ref.dtype)


def _dsa(z, pos, qk_g, tq, tk):
    b, s, _ = z.shape
    topk = min(DSA_TOPK_MAX, s // 4)
    gq = jnp.tile(qk_g[0], N_HEADS)[None, :]
    gk = jnp.zeros((1, LANE), F32).at[0, SM_KV:SM_KV + HEAD_DIM].set(qk_g[1])
    rope_q = _rope_rows(BRANCH_W, HEAD_DIM, HEAD_DIM // ROPE_FRAC)
    rope_i = _rope_rows(IDX_HEADS * IDX_D, IDX_D, IDX_D // ROPE_FRAC)
    rope_k = _dsa_key_rope_rows()
    gmat = _group_mean_matrix(BRANCH_W, HEAD_DIM)
    blk = BRANCH_W
    return pl.pallas_call(
        functools.partial(_dsa_kernel, tq=tq, tk=tk, seq=s, topk=float(topk),
                          idx_bits=max(1, (s - 1).bit_length())),
        grid=(b, s // tq),
        in_specs=[pl.BlockSpec((1, s, 1), lambda bi, qi: (bi, 0, 0)),
                  pl.BlockSpec((1, tq, blk), lambda bi, qi: (bi, qi, Z_DSA_Q // blk)),
                  pl.BlockSpec((1, tq, blk), lambda bi, qi: (bi, qi, Z_IDX_Q // blk)),
                  pl.BlockSpec((1, s, LANE), lambda bi, qi: (bi, 0, Z_SM_A // LANE)),
                  pl.BlockSpec((1, s, LANE), lambda bi, qi: (bi, 0, Z_SM_B // LANE)),
                  _full((1, blk)), _full((1, LANE)), _full(rope_q.shape), _full(rope_i.shape),
                  _full(rope_k.shape), _full(gmat.shape)],
        out_specs=pl.BlockSpec((1, tq, blk), lambda bi, qi: (bi, qi, 0)),
        out_shape=jax.ShapeDtypeStruct((b, s, blk), BF16),
        scratch_shapes=[pltpu.VMEM((s, LANE), BF16), pltpu.VMEM((s // tk, HEAD_DIM, tk), BF16),
                        pltpu.VMEM((s // tk, tk, tq), I32),
                        pltpu.VMEM((LANE, IDX_HEADS * tq), BF16), pltpu.VMEM((LANE, N_HEADS * tq), BF16),
                        pltpu.VMEM((tk, IDX_HEADS * tq), F32), pltpu.VMEM((tk, N_HEADS * tq), F32),
                        pltpu.VMEM((HEAD_DIM, N_HEADS * tq), F32)] + [pltpu.VMEM((s, LANE), F32)] * 4,
        compiler_params=_cparams(("parallel", "arbitrary")),
        name="dsa",
    )(pos, z, z, z, z, gq, gk, rope_q, rope_i, rope_k, gmat)


def _tiles(seq):
    return dict(tm=min(512, seq), tf=256, tq=min(256, seq), tk=min(256, seq), chunk=min(256, seq))


def kernel(x, positions, ffn1_norm, ffn1_w_gu, ffn1_w_down, mix_norm, w_in, diff_qk_norm, diff_lambda, diff_head_norm, ml_conv_w, ml_conv_b, ml_gate_bias, ml_head_norm, dsa_qk_norm, w_branch, w_out, ffn2_norm, ffn2_w_gu, ffn2_w_down):
    b, s, d = x.shape
    depth = w_in.shape[0]
    t = b * s
    tl = _tiles(s)
    pos = positions.astype(F32)[..., None]
    perm, gate_off = _in_col_permutation(d)
    keep = jnp.asarray(perm >= 0)
    src = jnp.asarray(np.maximum(perm, 0))
    x2 = x.reshape(t, d)
    for l in range(depth):
        x2 = _ffn(x2, ffn1_norm[l][None, :], ffn1_w_gu[l].astype(BF16), ffn1_w_down[l].astype(BF16), tl["tm"], tl["tf"])
        w_perm = jnp.where(keep[None, :], jnp.take(w_in[l], src, axis=1), 0.0).astype(BF16)
        z = _mixin(x2, mix_norm[l][None, :], w_perm, tl["tm"]).reshape(b, s, Z_COLS)
        outs = (
            _diff_attention(z, pos, diff_qk_norm[l], diff_lambda[l], diff_head_norm[l], l, tl["tq"], tl["tk"]),
            _mlstm(z, ml_conv_w[l], ml_conv_b[l], ml_gate_bias[l], ml_head_norm[l], tl["chunk"]),
            _stick_breaking(z, tl["tq"], tl["tk"]),
            _dsa(z, pos, dsa_qk_norm[l], tl["tq"], tl["tk"]),
        )
        wgate = w_in[l][:, gate_off:].reshape(d, N_BRANCH, d).transpose(1, 0, 2).astype(BF16)
        x2 = _merge(x2, mix_norm[l][None, :], [o.reshape(t, BRANCH_W) for o in outs], wgate,
                    w_branch[l].astype(BF16), w_out[l].astype(BF16), tl["tm"])
        x2 = _ffn(x2, ffn2_norm[l][None, :], ffn2_w_gu[l].astype(BF16), ffn2_w_down[l].astype(BF16), tl["tm"], tl["tf"])
    return x2.reshape(b, s, d)
```

```python
import functools
import math

import numpy as np
import jax
import jax.numpy as jnp
from jax import lax
from jax.experimental import pallas as pl
from jax.experimental.pallas import tpu as pltpu

F32 = jnp.float32
BF16 = jnp.bfloat16
I32 = jnp.int32

EPS = 1e-6
ROPE_THETA = 500000.0
ROPE_FRAC = 4
HEAD_DIM = 64
N_HEADS = 4
BRANCH_W = N_HEADS * HEAD_DIM
DIFF_DC = HEAD_DIM // 2
IDX_HEADS = 8
IDX_D = 32
DSA_TOPK_MAX = 256
CONV_W = 4
N_BRANCH = 4
NEG_BIG = -1e30
MASKED = -2e30
UNDERFLOW_LOG = -104.0
INT_MIN = -(2 ** 31)
LANE = 128
VMEM_LIMIT = 56 * 1024 * 1024

Z_DIFF_Q, Z_DIFF_K, Z_DIFF_V = 0, 256, 512
Z_ML_V, Z_ML_QK, Z_ML_O = 768, 1024, 1536
Z_SB_Q, Z_SB_K, Z_SB_V = 1792, 2048, 2304
Z_DSA_Q, Z_IDX_Q = 2560, 2816
Z_SM_A, Z_SM_B = 3072, 3200
Z_COLS = 3328
SM_GATE = 0
SM_IDXW = 4
SM_IDXK = 32
SM_KV = 64


def _in_col_permutation(d_model):
    src, off = {}, 0
    for name, w in (("diff_q", 256), ("diff_k", 256), ("diff_v", 256), ("ml_qk", 512), ("ml_v", 256),
                    ("ml_i", 4), ("ml_f", 4), ("ml_o", 256), ("sb_q", 256), ("sb_k", 256), ("sb_v", 256),
                    ("dsa_q", 256), ("dsa_k", 64), ("dsa_v", 64), ("idx_q", 256), ("idx_k", 32),
                    ("idx_w", 8), ("gates", N_BRANCH * d_model)):
        src[name] = (off, w)
        off += w
    perm = -np.ones((Z_COLS,), np.int64)

    def put(dst, name):
        o, w = src[name]
        perm[dst:dst + w] = np.arange(o, o + w)

    put(Z_DIFF_Q, "diff_q"); put(Z_DIFF_K, "diff_k"); put(Z_DIFF_V, "diff_v")
    put(Z_ML_QK, "ml_qk"); put(Z_ML_V, "ml_v"); put(Z_ML_O, "ml_o")
    put(Z_SB_Q, "sb_q"); put(Z_SB_K, "sb_k"); put(Z_SB_V, "sb_v")
    put(Z_DSA_Q, "dsa_q"); put(Z_IDX_Q, "idx_q")
    put(Z_SM_A + SM_GATE, "ml_i"); put(Z_SM_A + SM_IDXK, "idx_k"); put(Z_SM_A + SM_KV, "dsa_k")
    put(Z_SM_B + SM_GATE, "ml_f"); put(Z_SM_B + SM_IDXW, "idx_w"); put(Z_SM_B + SM_KV, "dsa_v")
    return perm, src["gates"][0]


def _rope_rows(width, group, rot_dim):
    half = rot_dim // 2
    inv = ROPE_THETA ** (-np.arange(half, dtype=np.float32) / half)
    r = np.arange(width) % group
    invf = np.where(r < rot_dim, inv[r % half], 0.0).astype(np.float32)
    m_lo = np.where(r < half, -1.0, 0.0).astype(np.float32)
    m_hi = np.where((r >= half) & (r < rot_dim), 1.0, 0.0).astype(np.float32)
    return jnp.asarray(np.stack([invf, m_lo, m_hi]))


def _group_mean_matrix(width, group):
    g = np.arange(width) // group
    return jnp.asarray((g[:, None] == g[None, :]).astype(np.float32) / group, dtype=BF16)


def _dot(a, b):
    return jnp.dot(a, b, preferred_element_type=F32)


def _dot_nt(a, b):
    return lax.dot_general(a, b, (((1,), (1,)), ((), ())), preferred_element_type=F32)


def _dot_split(a, b):
    hi = a.astype(BF16)
    lo = (a - hi.astype(F32)).astype(BF16)
    return _dot(hi, b) + _dot(lo, b)


def _rms(x, g):
    return x * lax.rsqrt(jnp.mean(x * x, axis=-1, keepdims=True) + EPS) * g


def _softplus(z):
    return jnp.maximum(z, 0.0) + jnp.log(1.0 + jnp.exp(-jnp.abs(z)))


def _group_norm(x, gmat, gain):
    ms = _dot_split(x * x, gmat)
    return x * lax.rsqrt(ms + EPS) * gain


def _rope_tables(pos, rope_rows):
    ang = pos * rope_rows[0:1, :LANE]
    return jnp.cos(ang), jnp.sin(ang)


def _rope(x, cos, sin, rope_rows, half):
    w = x.shape[-1]
    if w != LANE:
        cos = jnp.concatenate([cos] * (w // LANE), axis=1)
        sin = jnp.concatenate([sin] * (w // LANE), axis=1)
    partner = (pltpu.roll(x, w - half, 1) * rope_rows[1:2, :] + pltpu.roll(x, half, 1) * rope_rows[2:3, :])
    return x * cos + partner * sin


def _cparams(sem):
    return pltpu.CompilerParams(dimension_semantics=sem, vmem_limit_bytes=VMEM_LIMIT)


def _full(shape):
    n = len(shape)
    return pl.BlockSpec(shape, lambda *_: (0,) * n)


def _ffn_kernel(x_ref, g_ref, wgu_ref, wd_ref, o_ref, h_ref, a_ref, *, tf):
    d_ff = wd_ref.shape[0]
    h_ref[...] = _rms(x_ref[...], g_ref[...]).astype(BF16)
    for c in range(0, d_ff, tf):
        h = h_ref[...]
        g = _dot(h, wgu_ref[:, c:c + tf])
        u = _dot(h, wgu_ref[:, d_ff + c:d_ff + c + tf])
        a_ref[:, c:c + tf] = ((g * jax.nn.sigmoid(g)) * u).astype(BF16)
    o_ref[...] = x_ref[...] + 0.5 * _dot(a_ref[...], wd_ref[...])


def _ffn(x2, g, wgu, wd, tm, tf):
    t, d = x2.shape
    d_ff = wd.shape[0]
    assert d_ff % tf == 0 and wgu.shape == (d, 2 * d_ff)
    return pl.pallas_call(
        functools.partial(_ffn_kernel, tf=tf),
        grid=(t // tm,),
        in_specs=[pl.BlockSpec((tm, d), lambda i: (i, 0)), _full((1, d)), _full(wgu.shape), _full(wd.shape)],
        out_specs=pl.BlockSpec((tm, d), lambda i: (i, 0)),
        out_shape=jax.ShapeDtypeStruct((t, d), F32),
        scratch_shapes=[pltpu.VMEM((tm, d), BF16), pltpu.VMEM((tm, d_ff), BF16)],
        compiler_params=_cparams(("parallel",)),
        name="ffn",
    )(x2, g, wgu, wd)


def _mixin_kernel(x_ref, g_ref, w_ref, z_ref, *, col_chunk):
    h = _rms(x_ref[...], g_ref[...]).astype(BF16)
    for c in range(0, z_ref.shape[-1], col_chunk):
        z_ref[:, c:c + col_chunk] = _dot(h, w_ref[:, c:c + col_chunk])


def _mixin(x2, g, w_perm, tm):
    t, d = x2.shape
    n = w_perm.shape[1]
    return pl.pallas_call(
        functools.partial(_mixin_kernel, col_chunk=256),
        grid=(t // tm,),
        in_specs=[pl.BlockSpec((tm, d), lambda i: (i, 0)), _full((1, d)), _full((d, n))],
        out_specs=pl.BlockSpec((tm, n), lambda i: (i, 0)),
        out_shape=jax.ShapeDtypeStruct((t, n), F32),
        compiler_params=_cparams(("parallel",)),
        name="mix_in",
    )(x2, g, w_perm)


def _merge_kernel(x_ref, g_ref, o0_ref, o1_ref, o2_ref, o3_ref, wgate_ref, wbr_ref, wout_ref, out_ref):
    x = x_ref[...]
    h = _rms(x, g_ref[...]).astype(BF16)
    y = None
    for b, o_ref in enumerate((o0_ref, o1_ref, o2_ref, o3_ref)):
        gate = jax.nn.sigmoid(_dot(h, wgate_ref[b]))
        yb = gate * _dot(o_ref[...], wbr_ref[b])
        y = yb if y is None else y + yb
    out_ref[...] = x + _dot(y.astype(BF16), wout_ref[...])


def _merge(x2, g, outs, wgate, wbr, wout, tm):
    t, d = x2.shape
    bw = wbr.shape[1]
    o_spec = pl.BlockSpec((tm, bw), lambda i: (i, 0))
    return pl.pallas_call(
        _merge_kernel,
        grid=(t // tm,),
        in_specs=[pl.BlockSpec((tm, d), lambda i: (i, 0)), _full((1, d)), o_spec, o_spec, o_spec, o_spec,
                  _full(wgate.shape), _full(wbr.shape), _full(wout.shape)],
        out_specs=pl.BlockSpec((tm, d), lambda i: (i, 0)),
        out_shape=jax.ShapeDtypeStruct((t, d), F32),
        compiler_params=_cparams(("parallel",)),
        name="merge",
    )(x2, g, *outs, wgate, wbr, wout)


def _diff_kernel(pos_ref, zq_ref, zk_ref, zv_ref, gq_ref, gk_ref, lp_ref, hg_ref, rope_ref, gmat_ref,
                 o_ref, k_scr, vt_scr, q_scr, s_scr, acc_scr, cos_scr, sin_scr, *, tq, tk, seq, lam_init):
    assert tq == tk
    qi = pl.program_id(1)
    half = DIFF_DC // ROPE_FRAC // 2
    rope_rows = rope_ref[...]
    gmat = gmat_ref[...]

    @pl.when(qi == 0)
    def _prep():
        def body(c, carry):
            rows = pl.ds(pl.multiple_of(c * tk, tk), tk)
            k = _group_norm(zk_ref[0, rows, :], gmat, gk_ref[...])
            cos, sin = _rope_tables(pos_ref[0, rows, :], rope_rows)
            cos_scr[rows, :] = cos
            sin_scr[rows, :] = sin
            k_scr[rows, :] = _rope(k, cos, sin, rope_rows, half).astype(BF16)
            vt_scr[c] = zv_ref[0, rows, :].T.astype(BF16)
            return carry
        lax.fori_loop(0, seq // tk, body, 0)

    qrows = pl.ds(pl.multiple_of(qi * tq, tq), tq)
    q = _group_norm(zq_ref[0], gmat, gq_ref[...])
    q = _rope(q, cos_scr[qrows, :], sin_scr[qrows, :], rope_rows, half) * (DIFF_DC ** -0.5)
    q_t = q.T
    n_maps = 2 * N_HEADS
    grp = lax.shift_right_logical(lax.broadcasted_iota(I32, q_t.shape, 0), DIFF_DC.bit_length() - 1)
    for g in range(n_maps):
        q_scr[:, g * tq:(g + 1) * tq] = jnp.where(grp == g, q_t, 0.0).astype(BF16)

    lp = lp_ref[...]
    lam = (jnp.exp(jnp.sum(lp[0:1] * lp[1:2], axis=-1, keepdims=True))
           - jnp.exp(jnp.sum(lp[2:3] * lp[3:4], axis=-1, keepdims=True)) + lam_init)

    acc_scr[...] = jnp.zeros_like(acc_scr)

    def block(j, carry, diagonal):
        m_all, l_all = carry
        kblk = k_scr[pl.ds(pl.multiple_of(j * tk, tk), tk), :]
        for g in range(n_maps):
            s_scr[:, g * tq:(g + 1) * tq] = _dot(kblk, q_scr[:, g * tq:(g + 1) * tq])
        vt = vt_scr[j]
        if diagonal:
            mask = lax.broadcasted_iota(I32, (tk, tq), 0) <= lax.broadcasted_iota(I32, (tk, tq), 1)
        m_out, l_out = [], []
        for h in range(N_HEADS):
            ps, alphas = [], []
            for c in range(2):
                g = 2 * h + c
                m_old = m_all[:, g * tq:(g + 1) * tq]
                s = s_scr[:, g * tq:(g + 1) * tq]
                if diagonal:
                    s = jnp.where(mask, s, MASKED)
                m_new = jnp.maximum(m_old, jnp.max(s, axis=0, keepdims=True))
                alpha = jnp.exp(m_old - m_new)
                p = jnp.exp(s - m_new)
                l_out.append(alpha * l_all[:, g * tq:(g + 1) * tq] + jnp.sum(p, axis=0, keepdims=True))
                m_out.append(m_new)
                ps.append(p.astype(BF16))
                alphas.append(alpha)
            cols = slice(2 * h * tq, (2 * h + 2) * tq)
            pv = _dot(vt[h * HEAD_DIM:(h + 1) * HEAD_DIM, :], jnp.concatenate(ps, axis=1))
            acc_scr[:, cols] = acc_scr[:, cols] * jnp.concatenate(alphas, axis=1) + pv
        return jnp.concatenate(m_out, axis=1), jnp.concatenate(l_out, axis=1)

    init = (jnp.full((1, n_maps * tq), NEG_BIG, F32), jnp.zeros((1, n_maps * tq), F32))
    carry = lax.fori_loop(0, qi, functools.partial(block, diagonal=False), init)
    _, l_all = block(qi, carry, diagonal=True)

    outs = []
    for h in range(N_HEADS):
        a = acc_scr[:, 2 * h * tq:(2 * h + 2) * tq] / l_all[:, 2 * h * tq:(2 * h + 2) * tq]
        o = a[:, :tq] - lam * a[:, tq:]
        o = o * lax.rsqrt(jnp.mean(o * o, axis=0, keepdims=True) + EPS) * hg_ref[...] * (1.0 - lam_init)
        outs.append(o)
    o_ref[0] = jnp.concatenate(outs, axis=0).T.astype(o_ref.dtype)


def _diff_attention(z, pos, qk_g, lam_p, head_g, layer_idx, tq, tk):
    b, s, _ = z.shape
    lam_init = 0.8 - 0.6 * math.exp(-0.3 * layer_idx)
    reps = BRANCH_W // DIFF_DC
    gq = jnp.tile(qk_g[0], reps)[None, :]
    gk = jnp.tile(qk_g[1], reps)[None, :]
    rope_rows = _rope_rows(BRANCH_W, DIFF_DC, DIFF_DC // ROPE_FRAC)
    gmat = _group_mean_matrix(BRANCH_W, DIFF_DC)
    blk = BRANCH_W
    return pl.pallas_call(
        functools.partial(_diff_kernel, tq=tq, tk=tk, seq=s, lam_init=lam_init),
        grid=(b, s // tq),
        in_specs=[pl.BlockSpec((1, s, 1), lambda bi, qi: (bi, 0, 0)),
                  pl.BlockSpec((1, tq, blk), lambda bi, qi: (bi, qi, Z_DIFF_Q // blk)),
                  pl.BlockSpec((1, s, blk), lambda bi, qi: (bi, 0, Z_DIFF_K // blk)),
                  pl.BlockSpec((1, s, blk), lambda bi, qi: (bi, 0, Z_DIFF_V // blk)),
                  _full((1, blk)), _full((1, blk)), _full(lam_p.shape), _full((HEAD_DIM, 1)),
                  _full(rope_rows.shape), _full(gmat.shape)],
        out_specs=pl.BlockSpec((1, tq, blk), lambda bi, qi: (bi, qi, 0)),
        out_shape=jax.ShapeDtypeStruct((b, s, blk), BF16),
        scratch_shapes=[pltpu.VMEM((s, blk), BF16), pltpu.VMEM((s // tk, blk, tk), BF16),
                        pltpu.VMEM((blk, 2 * N_HEADS * tq), BF16), pltpu.VMEM((tk, 2 * N_HEADS * tq), F32),
                        pltpu.VMEM((HEAD_DIM, 2 * N_HEADS * tq), F32),
                        pltpu.VMEM((s, LANE), F32), pltpu.VMEM((s, LANE), F32)],
        compiler_params=_cparams(("parallel", "arbitrary")),
        name="diff_attn",
    )(pos, z, z, z, gq, gk, lam_p, head_g[:, None], rope_rows, gmat)


def _sb_kernel(zq_ref, zk_ref, zv_ref, o_ref, k_scr, vt_scr, q_scr, z_scr, acc_scr, *, tq, tk, seq):
    assert tq == tk
    qi = pl.program_id(1)

    @pl.when(qi == 0)
    def _prep():
        def body(c, carry):
            rows = pl.ds(pl.multiple_of(c * tk, tk), tk)
            k_scr[rows, :] = zk_ref[0, rows, :].astype(BF16)
            vt_scr[c] = zv_ref[0, rows, :].T.astype(BF16)
            return carry
        lax.fori_loop(0, seq // tk, body, 0)

    q_t = (zq_ref[0] * (HEAD_DIM ** -0.5)).T
    grp = lax.shift_right_logical(lax.broadcasted_iota(I32, q_t.shape, 0), HEAD_DIM.bit_length() - 1)
    for h in range(N_HEADS):
        q_scr[:, h * tq:(h + 1) * tq] = jnp.where(grp == h, q_t, 0.0).astype(BF16)
    later = (lax.broadcasted_iota(I32, (tk, tk), 1) > lax.broadcasted_iota(I32, (tk, tk), 0)).astype(BF16)
    acc_scr[...] = jnp.zeros_like(acc_scr)

    def block(j, tail_all, diagonal):
        kblk = k_scr[pl.ds(pl.multiple_of(j * tk, tk), tk), :]
        z_scr[...] = _dot(kblk, q_scr[...])
        vt = vt_scr[j]
        if diagonal:
            mask = lax.broadcasted_iota(I32, (tk, tq), 0) < lax.broadcasted_iota(I32, (tk, tq), 1)
        tails = []
        for h in range(N_HEADS):
            cols = slice(h * tq, (h + 1) * tq)
            z = z_scr[:, cols]
            if diagonal:
                z = jnp.where(mask, z, MASKED)
            sp = _softplus(z)
            hi = (-sp).astype(BF16)
            lo = (-sp - hi.astype(F32)).astype(BF16)
            both = _dot(later, jnp.concatenate([hi, lo], axis=1))
            inside = both[:, :tq] + both[:, tq:]
            a = jnp.exp(z - sp + inside + tail_all[:, cols])
            acc_scr[:, cols] += _dot(vt[h * HEAD_DIM:(h + 1) * HEAD_DIM, :], a.astype(BF16))
            tails.append(tail_all[:, cols] + inside[0:1, :] - sp[0:1, :])
        return jnp.concatenate(tails, axis=1)

    def alive_flag(tail_all):
        return (jnp.max(tail_all) > UNDERFLOW_LOG).astype(I32)

    def cond(carry):
        jj, alive, _ = carry
        return (jj < qi) & (alive > 0)

    def body(carry):
        jj, _, tail_all = carry
        tail_all = block(qi - 1 - jj, tail_all, diagonal=False)
        return jj + 1, alive_flag(tail_all), tail_all

    tail0 = block(qi, jnp.zeros((1, N_HEADS * tq), F32), diagonal=True)
    lax.while_loop(cond, body, (jnp.int32(0), alive_flag(tail0), tail0))
    o_ref[0] = jnp.concatenate([acc_scr[:, h * tq:(h + 1) * tq] for h in range(N_HEADS)], axis=0).T.astype(o_ref.dtype)


def _stick_breaking(z, tq, tk):
    b, s, _ = z.shape
    blk = BRANCH_W
    return pl.pallas_call(
        functools.partial(_sb_kernel, tq=tq, tk=tk, seq=s),
        grid=(b, s // tq),
        in_specs=[pl.BlockSpec((1, tq, blk), lambda bi, qi: (bi, qi, Z_SB_Q // blk)),
                  pl.BlockSpec((1, s, blk), lambda bi, qi: (bi, 0, Z_SB_K // blk)),
                  pl.BlockSpec((1, s, blk), lambda bi, qi: (bi, 0, Z_SB_V // blk))],
        out_specs=pl.BlockSpec((1, tq, blk), lambda bi, qi: (bi, qi, 0)),
        out_shape=jax.ShapeDtypeStruct((b, s, blk), BF16),
        scratch_shapes=[pltpu.VMEM((s, blk), BF16), pltpu.VMEM((s // tk, blk, tk), BF16),
                        pltpu.VMEM((blk, N_HEADS * tq), BF16), pltpu.VMEM((tk, N_HEADS * tq), F32),
                        pltpu.VMEM((HEAD_DIM, N_HEADS * tq), F32)],
        compiler_params=_cparams(("parallel", "arbitrary")),
        name="stick_breaking",
    )(z, z, z)


def _mlstm_kernel(zqk_ref, zv_ref, zo_ref, za_ref, zb_ref, cw_ref, cb_ref, bi_ref, bf_ref, hg_ref,
                  o_ref, xbuf, c_scr, m_scr, *, chunk):
    ci = pl.program_id(1)
    L = chunk
    pad = 8

    @pl.when(ci == 0)
    def _init():
        xbuf[0:pad, :] = jnp.zeros((pad, xbuf.shape[1]), F32)
        c_scr[...] = jnp.zeros_like(c_scr)
        m_scr[...] = jnp.zeros_like(m_scr)

    x = zqk_ref[0]
    xbuf[pad:pad + L, :] = x
    y = x * cw_ref[CONV_W - 1:CONV_W, :] + cb_ref[...]
    for d in range(1, CONV_W):
        y = y + xbuf[pad - d:pad - d + L, :] * cw_ref[CONV_W - 1 - d:CONV_W - d, :]
    xbuf[0:pad, :] = x[L - pad:L, :]
    qk = y * jax.nn.sigmoid(y)
    q_all = (qk[:, :BRANCH_W] * (HEAD_DIM ** -0.5)).astype(BF16)
    k_all = qk[:, BRANCH_W:]

    ig = za_ref[0] + bi_ref[...]
    fpre = zb_ref[0] + bf_ref[...]
    lf = -_softplus(-fpre)
    row = lax.broadcasted_iota(I32, (L, L), 0)
    col = lax.broadcasted_iota(I32, (L, L), 1)
    tri = row >= col
    bcum = _dot_split_left(tri.astype(BF16), lf)
    cmat = ig - bcum
    cmat_t = cmat.T

    v_all = zv_ref[0]
    lane = lax.broadcasted_iota(I32, (L, LANE), 1)
    ogate = jax.nn.sigmoid(zo_ref[0])

    for h in range(N_HEADS):
        lo = h * HEAD_DIM
        b_col = bcum[:, h:h + 1]
        c_row = cmat_t[h:h + 1, :]
        c_col = cmat[:, h:h + 1]
        m_old = m_scr[h:h + 1, 0:1]
        qh = q_all[:, lo:lo + HEAD_DIM]
        kh = k_all[:, lo:lo + HEAD_DIM]
        v2 = v_all[:, (h // 2) * LANE:(h // 2 + 1) * LANE]
        if h % 2 == 1:
            v2 = pltpu.roll(v2, HEAD_DIM, 1)
        v_aug = jnp.where(lane < HEAD_DIM, v2, (lane == HEAD_DIM).astype(F32)).astype(BF16)

        dl = jnp.where(tri, b_col + c_row, NEG_BIG)
        inter = b_col + m_old
        mt = jnp.maximum(inter, jnp.max(dl, axis=-1, keepdims=True))
        dw = jnp.exp(dl - mt)
        iw = jnp.exp(inter - mt)
        s = _dot_nt(qh, kh.astype(BF16)) * dw
        state = c_scr[h]
        qc = _dot(qh, state.astype(BF16))
        sv = _dot(s.astype(BF16), v_aug)
        num = iw * qc[:, :HEAD_DIM] + sv[:, :HEAD_DIM]
        den = iw * qc[:, HEAD_DIM:HEAD_DIM + 1] + jnp.sum(s, axis=-1, keepdims=True)
        hout = num / jnp.maximum(jnp.abs(den), jnp.exp(-mt))

        bl = b_col[L - 1:L, :]
        g = bl + c_col
        m_new = jnp.maximum(bl + m_old, jnp.max(g, axis=0, keepdims=True))
        decay = jnp.exp(bl + m_old - m_new)
        wk = jnp.exp(g - m_new)
        kw_t = _transpose_cols(kh * wk)
        c_scr[h] = decay * state + _dot(kw_t.astype(BF16), v_aug)
        m_scr[h:h + 1, :] = jnp.broadcast_to(m_new, (1, LANE))

        hn = _rms(hout, hg_ref[...])
        o_ref[0, :, lo:lo + HEAD_DIM] = (ogate[:, lo:lo + HEAD_DIM] * hn).astype(o_ref.dtype)


def _dot_split_left(a, b):
    hi = b.astype(BF16)
    lo = (b - hi.astype(F32)).astype(BF16)
    return _dot(a, hi) + _dot(a, lo)


def _transpose_cols(x):
    l, w = x.shape
    xp = jnp.concatenate([x, jnp.zeros((l, LANE - w), x.dtype)], axis=1)
    return xp.T[:w, :]


def _mlstm(z, conv_w, conv_b, gate_b, head_g, chunk):
    b, s, _ = z.shape
    bi = jnp.zeros((1, LANE), F32).at[0, SM_GATE:SM_GATE + N_HEADS].set(gate_b[0])
    bf = jnp.zeros((1, LANE), F32).at[0, SM_GATE:SM_GATE + N_HEADS].set(gate_b[1])
    wqk = 2 * BRANCH_W
    return pl.pallas_call(
        functools.partial(_mlstm_kernel, chunk=chunk),
        grid=(b, s // chunk),
        in_specs=[pl.BlockSpec((1, chunk, wqk), lambda bi_, ci: (bi_, ci, Z_ML_QK // wqk)),
                  pl.BlockSpec((1, chunk, BRANCH_W), lambda bi_, ci: (bi_, ci, Z_ML_V // BRANCH_W)),
                  pl.BlockSpec((1, chunk, BRANCH_W), lambda bi_, ci: (bi_, ci, Z_ML_O // BRANCH_W)),
                  pl.BlockSpec((1, chunk, LANE), lambda bi_, ci: (bi_, ci, Z_SM_A // LANE)),
                  pl.BlockSpec((1, chunk, LANE), lambda bi_, ci: (bi_, ci, Z_SM_B // LANE)),
                  _full(conv_w.shape), _full((1, wqk)), _full((1, LANE)), _full((1, LANE)), _full((1, HEAD_DIM))],
        out_specs=pl.BlockSpec((1, chunk, BRANCH_W), lambda bi_, ci: (bi_, ci, 0)),
        out_shape=jax.ShapeDtypeStruct((b, s, BRANCH_W), BF16),
        scratch_shapes=[pltpu.VMEM((chunk + 8, wqk), F32), pltpu.VMEM((N_HEADS, HEAD_DIM, LANE), F32),
                        pltpu.VMEM((8, LANE), F32)],
        compiler_params=_cparams(("parallel", "arbitrary")),
        name="mlstm",
    )(z, z, z, z, z, conv_w, conv_b[None, :], bi, bf, head_g[None, :])


def _dsa_key_rope_rows():
    lane = np.arange(LANE)
    rows = np.zeros((5, LANE), np.float32)
    for base, rot, mrow in ((SM_IDXK, IDX_D // ROPE_FRAC, 1), (SM_KV, HEAD_DIM // ROPE_FRAC, 3)):
        half = rot // 2
        inv = ROPE_THETA ** (-np.arange(half, dtype=np.float32) / half)
        r = lane - base
        inside = (r >= 0) & (r < rot)
        rows[0] = np.where(inside, inv[np.clip(r, 0, rot - 1) % half], rows[0])
        rows[mrow] = np.where((r >= 0) & (r < half), -1.0, 0.0)
        rows[mrow + 1] = np.where((r >= half) & (r < rot), 1.0, 0.0)
    return jnp.asarray(rows)


def _sortable_key(score):
    bits = lax.bitcast_convert_type(jnp.where(score == 0.0, 0.0, score), I32)
    return bits ^ (lax.shift_right_arithmetic(bits, 31) & 0x7FFFFFFF)


def _dsa_kernel(pos_ref, zq_ref, ziq_ref, za_ref, zb_ref, gq_ref, gk_ref, ropeq_ref, ropei_ref, ropek_ref,
                gmat_ref, o_ref, ka_scr, vt_scr, keys_scr, qi_scr, q_scr, r_scr, s_scr, acc_scr,
                cq_scr, sq_scr, ci_scr, si_scr, *, tq, tk, seq, topk, idx_bits):
    qi = pl.program_id(1)
    n_blk = (qi * tq + tq + tk - 1) // tk
    half_q = HEAD_DIM // ROPE_FRAC // 2
    half_i = IDX_D // ROPE_FRAC // 2

    @pl.when(qi == 0)
    def _prep():
        is_k = lax.broadcasted_iota(I32, (tk, LANE), 1) >= SM_KV
        rk = ropek_ref[...]

        def body(c, carry):
            rows = pl.ds(pl.multiple_of(c * tk, tk), tk)
            a = za_ref[0, rows, :]
            ms = jnp.sum(jnp.where(is_k, a * a, 0.0), axis=-1, keepdims=True) * (1.0 / HEAD_DIM)
            a = jnp.where(is_k, a * lax.rsqrt(ms + EPS) * gk_ref[...], a)
            ang = pos_ref[0, rows, :] * rk[0:1, :]
            partner = (pltpu.roll(a, LANE - half_i, 1) * rk[1:2, :] + pltpu.roll(a, half_i, 1) * rk[2:3, :]
                       + pltpu.roll(a, LANE - half_q, 1) * rk[3:4, :] + pltpu.roll(a, half_q, 1) * rk[4:5, :])
            ka_scr[rows, :] = (a * jnp.cos(ang) + partner * jnp.sin(ang)).astype(BF16)
            vt_scr[c] = zb_ref[0, rows, :].T[SM_KV:SM_KV + HEAD_DIM, :].astype(BF16)
            cq_scr[rows, :], sq_scr[rows, :] = _rope_tables(pos_ref[0, rows, :], ropeq_ref[...])
            ci_scr[rows, :], si_scr[rows, :] = _rope_tables(pos_ref[0, rows, :], ropei_ref[...])
            return carry
        lax.fori_loop(0, seq // tk, body, 0)

    q0 = pl.multiple_of(qi * tq, tq)
    qrows = pl.ds(q0, tq)

    q = _group_norm(zq_ref[0], gmat_ref[...], gq_ref[...])
    q_t = (_rope(q, cq_scr[qrows, :], sq_scr[qrows, :], ropeq_ref[...], half_q) * (HEAD_DIM ** -0.5)).T
    q_scr[...] = jnp.zeros_like(q_scr)
    for h in range(N_HEADS):
        q_scr[SM_KV:SM_KV + HEAD_DIM, h * tq:(h + 1) * tq] = q_t[h * HEAD_DIM:(h + 1) * HEAD_DIM, :].astype(BF16)
    qx_t = _rope(ziq_ref[0], ci_scr[qrows, :], si_scr[qrows, :], ropei_ref[...], half_i).T
    qi_scr[...] = jnp.zeros_like(qi_scr)
    for h in range(IDX_HEADS):
        qi_scr[SM_IDXK:SM_IDXK + IDX_D, h * tq:(h + 1) * tq] = qx_t[h * IDX_D:(h + 1) * IDX_D, :].astype(BF16)
    w_t = zb_ref[0, pl.ds(q0, tq), :].T

    key_i = lax.broadcasted_iota(I32, (tk, tq), 0)
    qry_i = lax.broadcasted_iota(I32, (tk, tq), 1) + qi * tq

    def score_body(j, carry):
        kb = ka_scr[pl.ds(pl.multiple_of(j * tk, tk), tk), :]
        for h in range(IDX_HEADS):
            r_scr[:, h * tq:(h + 1) * tq] = _dot(kb, qi_scr[:, h * tq:(h + 1) * tq])
        score = None
        for h in range(IDX_HEADS):
            part = jnp.maximum(r_scr[:, h * tq:(h + 1) * tq], 0.0) * w_t[SM_IDXW + h:SM_IDXW + h + 1, :]
            score = part if score is None else score + part
        causal = (key_i + j * tk) <= qry_i
        keys_scr[j] = jnp.where(causal, _sortable_key(score), INT_MIN)
        return carry
    lax.fori_loop(0, n_blk, score_body, 0)
    keys_scr[n_blk] = jnp.full((tk, tq), INT_MIN, I32)

    def count(pred):
        def body(j, acc):
            ones = jnp.where(pred(keys_scr[j], j), 1.0, 0.0)
            return acc + jnp.sum(ones.reshape(tk // 8, 8, tq), axis=0)
        def body2(i, acc):
            return body(2 * i + 1, body(2 * i, acc))
        acc = lax.fori_loop(0, (n_blk + 1) // 2, body2, jnp.zeros((8, tq), F32))
        return jnp.sum(acc, axis=0, keepdims=True)

    n_causal = qry_i[0:1, :] + 1
    settled0 = jnp.where(n_causal <= topk, 1.0, 0.0)

    def thr_cond(carry):
        it, pending, _, _ = carry
        return (it < 32) & (pending > 0)

    bits_per_check = 4

    def thr_body(carry):
        it, _, t_u, settled = carry
        for k in range(bits_per_check):
            cand_u = t_u | lax.shift_left(jnp.int32(1), 31 - k - it)
            cand = cand_u ^ INT_MIN
            cnt = count(lambda kb, j, cand=cand: kb >= cand)
            t_u = jnp.where(cnt >= topk, cand_u, t_u)
            settled = jnp.where(cnt == topk, 1.0, settled)
        return it + bits_per_check, (jnp.min(settled) < 0.5).astype(I32), t_u, settled

    _, pending, t_u, _ = lax.while_loop(
        thr_cond, thr_body, (jnp.int32(0), (jnp.min(settled0) < 0.5).astype(I32), jnp.zeros((1, tq), I32), settled0))
    thr = t_u ^ INT_MIN
    has_thr = thr > INT_MIN

    @pl.when(pending > 0)
    def _break_ties():
        n_gt = count(lambda kb, j: kb > thr)
        n_ge = count(lambda kb, j: kb >= thr)
        need = topk - n_gt
        excess = has_thr & ((n_ge - n_gt) > need)

        def body(it, x):
            cand = x | lax.shift_left(jnp.int32(1), idx_bits - 1 - it)
            below = count(lambda kb, j: (kb == thr) & ((key_i + j * tk) < cand))
            return jnp.where(below < need, cand, x)
        last = lax.fori_loop(0, idx_bits, body, jnp.zeros((1, tq), I32))
        limit = jnp.where(excess, last, seq)

        def demote(j, carry):
            kb = keys_scr[j]
            keys_scr[j] = jnp.where((kb == thr) & ((key_i + j * tk) > limit), INT_MIN, kb)
            return carry
        lax.fori_loop(0, n_blk, demote, 0)

    thr_sel = jnp.where(has_thr, thr, INT_MIN + 1)
    acc_scr[...] = jnp.zeros_like(acc_scr)

    def att_body(j, carry):
        m_all, l_all = carry
        kb = ka_scr[pl.ds(pl.multiple_of(j * tk, tk), tk), :]
        for h in range(N_HEADS):
            s_scr[:, h * tq:(h + 1) * tq] = _dot(kb, q_scr[:, h * tq:(h + 1) * tq])
        sel = keys_scr[j] >= thr_sel
        vt = vt_scr[j]
        m_out, l_out = [], []
        for h in range(N_HEADS):
            cols = slice(h * tq, (h + 1) * tq)
            m_old = m_all[:, cols]
            s = jnp.where(sel, s_scr[:, cols], MASKED)
            m_new = jnp.maximum(m_old, jnp.max(s, axis=0, keepdims=True))
            alpha = jnp.exp(m_old - m_new)
            p = jnp.exp(s - m_new)
            l_out.append(alpha * l_all[:, cols] + jnp.sum(p, axis=0, keepdims=True))
            m_out.append(m_new)
            acc_scr[:, cols] = acc_scr[:, cols] * alpha + _dot(vt, p.astype(BF16))
        return jnp.concatenate(m_out, axis=1), jnp.concatenate(l_out, axis=1)

    init = (jnp.full((1, N_HEADS * tq), NEG_BIG, F32), jnp.zeros((1, N_HEADS * tq), F32))
    _, l_all = lax.fori_loop(0, n_blk, att_body, init)
    out = acc_scr[...] / l_all
    o_ref[0] = jnp.concatenate([out[:, h * tq:(h + 1) * tq] for h in range(N_HEADS)], axis=0).T.astype(o_ref.dtype)


def _dsa(z, pos, qk_g, tq, tk):
    b, s, _ = z.shape
    topk = min(DSA_TOPK_MAX, s // 4)
    gq = jnp.tile(qk_g[0], N_HEADS)[None, :]
    gk = jnp.zeros((1, LANE), F32).at[0, SM_KV:SM_KV + HEAD_DIM].set(qk_g[1])
    rope_q = _rope_rows(BRANCH_W, HEAD_DIM, HEAD_DIM // ROPE_FRAC)
    rope_i = _rope_rows(IDX_HEADS * IDX_D, IDX_D, IDX_D // ROPE_FRAC)
    rope_k = _dsa_key_rope_rows()
    gmat = _group_mean_matrix(BRANCH_W, HEAD_DIM)
    blk = BRANCH_W
    return pl.pallas_call(
        functools.partial(_dsa_kernel, tq=tq, tk=tk, seq=s, topk=float(topk),
                          idx_bits=max(1, (s - 1).bit_length())),
        grid=(b, s // tq),
        in_specs=[pl.BlockSpec((1, s, 1), lambda bi, qi: (bi, 0, 0)),
                  pl.BlockSpec((1, tq, blk), lambda bi, qi: (bi, qi, Z_DSA_Q // blk)),
                  pl.BlockSpec((1, tq, blk), lambda bi, qi: (bi, qi, Z_IDX_Q // blk)),
                  pl.BlockSpec((1, s, LANE), lambda bi, qi: (bi, 0, Z_SM_A // LANE)),
                  pl.BlockSpec((1, s, LANE), lambda bi, qi: (bi, 0, Z_SM_B // LANE)),
                  _full((1, blk)), _full((1, LANE)), _full(rope_q.shape), _full(rope_i.shape),
                  _full(rope_k.shape), _full(gmat.shape)],
        out_specs=pl.BlockSpec((1, tq, blk), lambda bi, qi: (bi, qi, 0)),
        out_shape=jax.ShapeDtypeStruct((b, s, blk), BF16),
        scratch_shapes=[pltpu.VMEM((s, LANE), BF16), pltpu.VMEM((s // tk, HEAD_DIM, tk), BF16),
                        pltpu.VMEM((s // tk + 1, tk, tq), I32),
                        pltpu.VMEM((LANE, IDX_HEADS * tq), BF16), pltpu.VMEM((LANE, N_HEADS * tq), BF16),
                        pltpu.VMEM((tk, IDX_HEADS * tq), F32), pltpu.VMEM((tk, N_HEADS * tq), F32),
                        pltpu.VMEM((HEAD_DIM, N_HEADS * tq), F32)] + [pltpu.VMEM((s, LANE), F32)] * 4,
        compiler_params=_cparams(("parallel", "arbitrary")),
        name="dsa",
    )(pos, z, z, z, z, gq, gk, rope_q, rope_i, rope_k, gmat)


def _tiles(seq):
    return dict(tm=min(512, seq), tf=256, tq=min(256, seq), tk=min(256, seq), chunk=min(256, seq))


def kernel(x, positions, ffn1_norm, ffn1_w_gu, ffn1_w_down, mix_norm, w_in, diff_qk_norm, diff_lambda, diff_head_norm, ml_conv_w, ml_conv_b, ml_gate_bias, ml_head_norm, dsa_qk_norm, w_branch, w_out, ffn2_norm, ffn2_w_gu, ffn2_w_down):
    b, s, d = x.shape
    depth = w_in.shape[0]
    t = b * s
    tl = _tiles(s)
    pos = positions.astype(F32)[..., None]
    perm, gate_off = _in_col_permutation(d)
    keep = jnp.asarray(perm >= 0)
    src = jnp.asarray(np.maximum(perm, 0))
    x2 = x.reshape(t, d)
    for l in range(depth):
        x2 = _ffn(x2, ffn1_norm[l][None, :], ffn1_w_gu[l].astype(BF16), ffn1_w_down[l].astype(BF16), tl["tm"], tl["tf"])
        w_perm = jnp.where(keep[None, :], jnp.take(w_in[l], src, axis=1), 0.0).astype(BF16)
        z = _mixin(x2, mix_norm[l][None, :], w_perm, tl["tm"]).reshape(b, s, Z_COLS)
        outs = (
            _diff_attention(z, pos, diff_qk_norm[l], diff_lambda[l], diff_head_norm[l], l, tl["tq"], tl["tk"]),
            _mlstm(z, ml_conv_w[l], ml_conv_b[l], ml_gate_bias[l], ml_head_norm[l], tl["chunk"]),
            _stick_breaking(z, tl["tq"], tl["tk"]),
            _dsa(z, pos, dsa_qk_norm[l], tl["tq"], tl["tk"]),
        )
        wgate = w_in[l][:, gate_off:].reshape(d, N_BRANCH, d).transpose(1, 0, 2).astype(BF16)
        x2 = _merge(x2, mix_norm[l][None, :], [o.reshape(t, BRANCH_W) for o in outs], wgate,
                    w_branch[l].astype(BF16), w_out[l].astype(BF16), tl["tm"])
        x2 = _ffn(x2, ffn2_norm[l][None, :], ffn2_w_gu[l].astype(BF16), ffn2_w_down[l].astype(BF16), tl["tm"], tl["tf"])
    return x2.reshape(b, s, d)
```
